```python
import math
import jax, jax.numpy as jnp
from jax import lax
import numpy as np

D_MODEL = 1024
BATCH = 2
SEQ = 8192
DEPTH = 2

GRID_W = 64
EPS = 1e-6
ROPE_THETA = 10000.0
BLOCK = 128
A_HEADS = 4
A_DK = 128
A_DV = 128
A_CONV = 5
A_CHUNK = 64
A_CONV_CH = 2 * A_HEADS * A_DK + A_HEADS * A_DV
HD = 64
B_HEADS = 4
B_KV = 2
WINDOW = 128
C_HEADS = 4
C_KV = 2
PROJ_SIZES = (A_CONV_CH, A_HEADS * A_DV, 2 * A_HEADS, 2 * A_HEADS,
              B_HEADS * HD, B_KV * HD, B_KV * HD,
              C_HEADS * HD, C_KV * HD, C_KV * HD)
PROJ_DIM = 3088
MIX_WIDTH = A_HEADS * A_DV + B_HEADS * HD + C_HEADS * HD
N_EXPERTS = 16
CAP_FACTOR = 2
D_EXPERT = 1024

kernel_name = "hybrid_parallel_heads_ec_moe_encoder"


def rmsnorm(x, g):
    xf = x.astype(jnp.float32)
    y = xf * lax.rsqrt(jnp.mean(xf * xf, axis=-1, keepdims=True) + EPS)
    return (y * g.astype(jnp.float32)).astype(x.dtype)


def l2norm(x):
    xf = x.astype(jnp.float32)
    return xf * lax.rsqrt(jnp.sum(xf * xf, axis=-1, keepdims=True) + EPS)


def rope_tables(pos, dim):
    inv = ROPE_THETA ** (-jnp.arange(0, dim, 2, dtype=jnp.float32) / dim)
    ang = pos.astype(jnp.float32)[:, None] * inv[None, :]
    ang = jnp.concatenate([ang, ang], axis=-1)
    return jnp.cos(ang), jnp.sin(ang)


def apply_rope(x, cos, sin):
    xf = x.astype(jnp.float32)
    half = xf.shape[-1] // 2
    rot = jnp.concatenate([-xf[..., half:], xf[..., :half]], axis=-1)
    return (xf * cos[None, :, None, :] + rot * sin[None, :, None, :]).astype(x.dtype)


def apply_axial_rope(x, row_cs, col_cs):
    h = x.shape[-1] // 2
    return jnp.concatenate([apply_rope(x[..., :h], *row_cs),
                            apply_rope(x[..., h:], *col_cs)], axis=-1)


def short_conv(x, w):
    k = w.shape[0]
    return lax.conv_general_dilated(
        x, w[:, None, :].astype(x.dtype), window_strides=(1,),
        padding=[(k // 2, k // 2)], dimension_numbers=('NWC', 'WIO', 'NWC'),
        feature_group_count=x.shape[-1])


def delta_rule_chunked(q, k, v, g, beta):
    B, S, H, dk = q.shape
    dv = v.shape[-1]
    n = S // A_CHUNK

    def chunks(t):
        return t.reshape(B, n, A_CHUNK, H, -1).transpose(1, 0, 3, 2, 4)

    q, k, v = chunks(q), chunks(k), chunks(v)
    g = g.reshape(B, n, A_CHUNK, H).transpose(1, 0, 3, 2)
    beta = beta.reshape(B, n, A_CHUNK, H).transpose(1, 0, 3, 2)
    gc = jnp.cumsum(g, axis=-1)
    tril = jnp.tril(jnp.ones((A_CHUNK, A_CHUNK), bool))
    strict = jnp.tril(jnp.ones((A_CHUNK, A_CHUNK), bool), -1)
    decay = jnp.exp(jnp.where(tril, gc[..., :, None] - gc[..., None, :], -jnp.inf))
    k_beta = k * beta[..., None]
    v_beta = v * beta[..., None]
    lower = jnp.where(strict, jnp.einsum('nbhid,nbhjd->nbhij', k_beta, k) * decay, 0.0)
    eye = jnp.eye(A_CHUNK, dtype=jnp.float32)
    t_mat = lax.linalg.triangular_solve(eye + lower, jnp.broadcast_to(eye, lower.shape),
                                        left_side=True, lower=True, unit_diagonal=True)
    u = jnp.einsum('nbhij,nbhjd->nbhid', t_mat, v_beta)
    w = jnp.einsum('nbhij,nbhjd->nbhid', t_mat, k_beta * jnp.exp(gc)[..., None])
    a_intra = jnp.einsum('nbhid,nbhjd->nbhij', q, k) * decay

    def step(state, inp):
        q_i, k_i, u_i, w_i, a_i, gc_i = inp
        v_new = u_i - jnp.einsum('bhcd,bhde->bhce', w_i, state)
        o = (jnp.einsum('bhcd,bhde->bhce', q_i * jnp.exp(gc_i)[..., None], state)
             + jnp.einsum('bhij,bhje->bhie', a_i, v_new))
        g_last = gc_i[..., -1]
        k_dec = k_i * jnp.exp(g_last[..., None] - gc_i)[..., None]
        state = state * jnp.exp(g_last)[..., None, None] + jnp.einsum('bhcd,bhce->bhde', k_dec, v_new)
        return state, o

    init = jnp.zeros((B, H, dk, dv), jnp.float32)
    _, o = lax.scan(step, init, (q, k, u, w, a_intra, gc))
    return o.transpose(1, 0, 3, 2, 4).reshape(B, S, H, dv)


def gated_deltanet(qkv, z, b, a, conv_w, a_log, dt_bias, norm_g):
    B, S, _ = qkv.shape
    qkv = jax.nn.silu(short_conv(qkv, conv_w))
    q, k, v = jnp.split(qkv, [A_HEADS * A_DK, 2 * A_HEADS * A_DK], axis=-1)
    q = l2norm(q.reshape(B, S, A_HEADS, A_DK)) * (A_DK ** -0.5)
    k = l2norm(k.reshape(B, S, A_HEADS, A_DK))
    v = v.reshape(B, S, A_HEADS, A_DV).astype(jnp.float32)
    beta = jax.nn.sigmoid(b.astype(jnp.float32)).reshape(B, S, 2, A_HEADS)
    g = (-jnp.exp(a_log.astype(jnp.float32))[None, None]
         * jax.nn.softplus(a.astype(jnp.float32).reshape(B, S, 2, A_HEADS)
                           + dt_bias.astype(jnp.float32)[None, None]))
    o_fwd = delta_rule_chunked(q, k, v, g[:, :, 0], beta[:, :, 0])
    fl = lambda t: jnp.flip(t, axis=1)
    o_bwd = fl(delta_rule_chunked(fl(q), fl(k), fl(v), fl(g[:, :, 1]), fl(beta[:, :, 1])))
    o = rmsnorm(o_fwd + o_bwd, norm_g) * jax.nn.silu(z.astype(jnp.float32).reshape(B, S, A_HEADS, A_DV))
    return o.reshape(B, S, A_HEADS * A_DV).astype(qkv.dtype)


def window_attention(q, k, v, sink):
    B, S, hq, d = q.shape
    hkv = k.shape[2]
    grp = hq // hkv
    nb = S // BLOCK
    qb = q.reshape(B, nb, BLOCK, hkv, grp, d)
    pad = ((0, 0), (BLOCK, BLOCK), (0, 0), (0, 0))
    kp = jnp.pad(k, pad).reshape(B, nb + 2, BLOCK, hkv, d)
    vp = jnp.pad(v, pad).reshape(B, nb + 2, BLOCK, hkv, d)
    kn = jnp.concatenate([kp[:, :-2], kp[:, 1:-1], kp[:, 2:]], axis=2)
    vn = jnp.concatenate([vp[:, :-2], vp[:, 1:-1], vp[:, 2:]], axis=2)
    s = jnp.einsum('bnqhgd,bnkhd->bnhgqk', qb, kn).astype(jnp.float32) * (d ** -0.5)
    blk = jnp.arange(nb)[:, None]
    qpos = blk * BLOCK + jnp.arange(BLOCK)[None, :]
    kpos = (blk - 1) * BLOCK + jnp.arange(3 * BLOCK)[None, :]
    valid = ((jnp.abs(qpos[:, :, None] - kpos[:, None, :]) <= WINDOW)
             & (kpos >= 0)[:, None, :] & (kpos < S)[:, None, :])
    s = jnp.where(valid[None, :, None, None], s, -jnp.inf)
    sink_b = sink.astype(jnp.float32).reshape(hkv, grp)[None, None, :, :, None, None]
    m = jnp.maximum(jnp.max(s, axis=-1, keepdims=True), sink_b)
    p = jnp.exp(s - m)
    p = p / (jnp.sum(p, axis=-1, keepdims=True) + jnp.exp(sink_b - m))
    o = jnp.einsum('bnhgqk,bnkhd->bnqhgd', p.astype(v.dtype), vn)
    return o.reshape(B, S, hq * d)


def dense_attention(q, k, v):
    B, S, hq, d = q.shape
    hkv = k.shape[2]
    grp = hq // hkv
    nb = S // BLOCK
    qb = q.reshape(B, nb, BLOCK, hkv, grp, d).transpose(1, 0, 2, 3, 4, 5)

    def one_block(q_blk):
        s = jnp.einsum('bqhgd,bkhd->bhgqk', q_blk, k).astype(jnp.float32) * (d ** -0.5)
        p = jax.nn.softmax(s, axis=-1)
        return jnp.einsum('bhgqk,bkhd->bqhgd', p.astype(v.dtype), v)

    o = lax.map(one_block, qb)
    return o.transpose(1, 0, 2, 3, 4, 5).reshape(B, S, hq * d)


def expert_choice_ffn(h, w_router, w_gate, w_up, w_down):
    B, S, D = h.shape
    cap = CAP_FACTOR * S // N_EXPERTS
    aff = jax.nn.softmax(jnp.einsum('bsd,de->bse', h, w_router).astype(jnp.float32), axis=-1)
    gate, idx = lax.top_k(jnp.swapaxes(aff, 1, 2), cap)
    bidx = jnp.arange(B)[:, None, None]
    xs = h[bidx, idx]
    hid = (jax.nn.silu(jnp.einsum('becd,edf->becf', xs, w_gate))
           * jnp.einsum('becd,edf->becf', xs, w_up))
    y = jnp.einsum('becf,efd->becd', hid, w_down) * gate[..., None].astype(h.dtype)
    return jnp.zeros_like(h).at[bidx, idx].add(y)


def setup_inputs(seed: int = 0) -> dict:
    key = jax.random.key(seed)
    ks = jax.random.split(key, 20)
    f32 = jnp.float32

    def nrm(k, shape, scale):
        return jax.random.normal(k, shape, f32) * scale

    x = nrm(ks[0], (BATCH, SEQ, D_MODEL), 1.0)
    norm1_g = 1.0 + nrm(ks[1], (DEPTH, D_MODEL), 0.02)
    w_in = nrm(ks[2], (DEPTH, D_MODEL, PROJ_DIM), D_MODEL ** -0.5)
    conv_w = nrm(ks[3], (DEPTH, A_CONV, A_CONV_CH), A_CONV ** -0.5)
    a_log = jnp.log(jax.random.uniform(ks[4], (DEPTH, 2, A_HEADS), f32, 1.0, 16.0))
    dt = jnp.exp(jax.random.uniform(ks[5], (DEPTH, 2, A_HEADS), f32, math.log(1e-3), math.log(1e-1)))
    dt_bias = dt + jnp.log(-jnp.expm1(-dt))
    a_norm_g = 1.0 + nrm(ks[6], (DEPTH, A_DV), 0.02)
    b_qnorm_g = 1.0 + nrm(ks[7], (DEPTH, HD), 0.02)
    b_knorm_g = 1.0 + nrm(ks[8], (DEPTH, HD), 0.02)
    b_sink = nrm(ks[9], (DEPTH, B_HEADS), 0.5)
    c_qnorm_g = 1.0 + nrm(ks[10], (DEPTH, HD), 0.02)
    c_knorm_g = 1.0 + nrm(ks[11], (DEPTH, HD), 0.02)
    w_out = nrm(ks[12], (DEPTH, MIX_WIDTH, D_MODEL), MIX_WIDTH ** -0.5)
    norm2_g = 1.0 + nrm(ks[13], (DEPTH, D_MODEL), 0.02)
    w_router = nrm(ks[14], (DEPTH, D_MODEL, N_EXPERTS), D_MODEL ** -0.5)
    w_gate = nrm(ks[15], (DEPTH, N_EXPERTS, D_MODEL, D_EXPERT), D_MODEL ** -0.5)
    w_up = nrm(ks[16], (DEPTH, N_EXPERTS, D_MODEL, D_EXPERT), D_MODEL ** -0.5)
    w_down = nrm(ks[17], (DEPTH, N_EXPERTS, D_EXPERT, D_MODEL), D_EXPERT ** -0.5)
    return {"x": x, "norm1_g": norm1_g, "w_in": w_in, "conv_w": conv_w,
            "a_log": a_log, "dt_bias": dt_bias, "a_norm_g": a_norm_g,
            "b_qnorm_g": b_qnorm_g, "b_knorm_g": b_knorm_g, "b_sink": b_sink,
            "c_qnorm_g": c_qnorm_g, "c_knorm_g": c_knorm_g, "w_out": w_out,
            "norm2_g": norm2_g, "w_router": w_router, "w_gate": w_gate,
            "w_up": w_up, "w_down": w_down}


def reference(x, norm1_g, w_in, conv_w, a_log, dt_bias, a_norm_g, b_qnorm_g, b_knorm_g,
              b_sink, c_qnorm_g, c_knorm_g, w_out, norm2_g, w_router, w_gate, w_up, w_down):
    B, S, _ = x.shape
    n_rows = S // GRID_W
    pos = jnp.arange(S)
    row_ids = jnp.repeat(jnp.arange(n_rows), GRID_W)
    col_ids = jnp.tile(jnp.arange(GRID_W), n_rows)
    rope_1d = rope_tables(pos, HD)
    rope_row = rope_tables(row_ids, HD // 2)
    rope_col = rope_tables(col_ids, HD // 2)
    split_idx = [int(i) for i in np.cumsum(PROJ_SIZES)[:-1]]

    for l in range(DEPTH):
        h = rmsnorm(x, norm1_g[l])
        proj = jnp.einsum('bsd,dp->bsp', h, w_in[l])
        qkv_a, z_a, b_a, a_a, q_b, k_b, v_b, q_c, k_c, v_c = jnp.split(proj, split_idx, axis=-1)

        out_a = gated_deltanet(qkv_a, z_a, b_a, a_a, conv_w[l], a_log[l], dt_bias[l], a_norm_g[l])

        qb = apply_rope(rmsnorm(q_b.reshape(B, S, B_HEADS, HD), b_qnorm_g[l]), *rope_1d)
        kb = apply_rope(rmsnorm(k_b.reshape(B, S, B_KV, HD), b_knorm_g[l]), *rope_1d)
        out_b = window_attention(qb, kb, v_b.reshape(B, S, B_KV, HD), b_sink[l])

        qc = apply_axial_rope(rmsnorm(q_c.reshape(B, S, C_HEADS, HD), c_qnorm_g[l]), rope_row, rope_col)
        kc = apply_axial_rope(rmsnorm(k_c.reshape(B, S, C_KV, HD), c_knorm_g[l]), rope_row, rope_col)
        out_c = dense_attention(qc, kc, v_c.reshape(B, S, C_KV, HD))

        mixed = jnp.concatenate([out_a, out_b, out_c], axis=-1)
        x = x + jnp.einsum('bsm,md->bsd', mixed, w_out[l])
        x = x + expert_choice_ffn(rmsnorm(x, norm2_g[l]), w_router[l], w_gate[l], w_up[l], w_down[l])
    return x
```

```python
import functools
import math

import jax
import jax.numpy as jnp
from jax import lax
from jax.experimental import pallas as pl
from jax.experimental.pallas import tpu as pltpu

F32 = jnp.float32
BF16 = jnp.bfloat16

D_MODEL = 1024
GRID_W = 64
EPS = 1e-6
ROPE_THETA = 10000.0
A_HEADS = 4
A_DK = 128
A_DV = 128
A_CONV = 5
A_CHUNK = 64
HD = 64
N_Q_HEADS = 4
N_KV_HEADS = 2
WINDOW = 128
N_EXPERTS = 16
CAP_FACTOR = 2
D_EXPERT = 1024

QKV_A = 3 * A_HEADS * A_DK
PA_W = QKV_A + A_HEADS * A_DV
ATT_W = (N_Q_HEADS + 2 * N_KV_HEADS) * HD
QK_W = (N_Q_HEADS + N_KV_HEADS) * HD
OFF_B = PA_W
OFF_C = PA_W + ATT_W
OFF_G = PA_W + 2 * ATT_W
LANE = 128
PROJ_W = OFF_G + LANE

VMEM_LIMIT = 56 * 1024 * 1024


def _params(sem, vmem=None):
    return pltpu.CompilerParams(dimension_semantics=sem,
                                vmem_limit_bytes=vmem or VMEM_LIMIT)


def _group_sumsq(v, width):
    n = v.shape[-1]
    r = lax.broadcasted_iota(jnp.int32, (n, n), 0) // width
    c = lax.broadcasted_iota(jnp.int32, (n, n), 1) // width
    ones = jnp.where(r == c, 1.0, 0.0).astype(BF16)
    sq = v * v
    hi = sq.astype(BF16)
    lo = (sq - hi.astype(F32)).astype(BF16)
    return (jnp.dot(hi, ones, preferred_element_type=F32)
            + jnp.dot(lo, ones, preferred_element_type=F32))


def _dup_halves(v):
    lane = lax.broadcasted_iota(jnp.int32, v.shape, 1)
    sw = pltpu.roll(v, 64, 1)
    lo = lane < 64
    return jnp.concatenate([jnp.where(lo, v, sw), jnp.where(lo, sw, v)], axis=1)


def _inproj_kernel(x_ref, g1_ref, w_ref, gain_b_ref, gain_c_ref,
                   cosb_ref, sinb_ref, cosc_ref, sinc_ref,
                   pa_ref, gb_ref, qb_ref, kb_ref, vb_ref, qc_ref, kc_ref, vc_ref):
    x = x_ref[...]
    ms = jnp.mean(x * x, axis=-1, keepdims=True)
    h = (x * lax.rsqrt(ms + EPS) * g1_ref[...]).astype(BF16)
    pa_ref[...] = jnp.dot(h, w_ref[:, 0:PA_W], preferred_element_type=F32)
    gb_ref[...] = jnp.dot(h, w_ref[:, OFF_G:PROJ_W], preferred_element_type=F32)

    def softmax_mixer(off, gain_ref, cos_ref, sin_ref, half, q_ref, k_ref, v_ref):
        y = jnp.dot(h, w_ref[:, off:off + ATT_W], preferred_element_type=F32)
        qk = y[:, :QK_W]
        ss = _group_sumsq(qk, HD)
        n = qk * lax.rsqrt(ss * (1.0 / HD) + EPS) * gain_ref[...]
        cos = jnp.concatenate([cos_ref[...]] * (QK_W // LANE), axis=1)
        sin = jnp.concatenate([sin_ref[...]] * (QK_W // LANE), axis=1)
        lane = lax.broadcasted_iota(jnp.int32, n.shape, 1)
        first = (lane % (2 * half)) < half
        rot = jnp.where(first, pltpu.roll(n, QK_W - half, 1), pltpu.roll(n, half, 1))
        r = n * cos + rot * sin
        q_ref[...] = r[:, :N_Q_HEADS * HD].astype(BF16)
        k_ref[...] = _dup_halves(r[:, N_Q_HEADS * HD:QK_W]).astype(BF16)
        v_ref[...] = _dup_halves(y[:, QK_W:ATT_W]).astype(BF16)

    softmax_mixer(OFF_B, gain_b_ref, cosb_ref, sinb_ref, HD // 2, qb_ref, kb_ref, vb_ref)
    softmax_mixer(OFF_C, gain_c_ref, cosc_ref, sinc_ref, HD // 4, qc_ref, kc_ref, vc_ref)


def _inproj(x2, g1, w_re, gain_b, gain_c, tabs, seq, tm):
    t = x2.shape[0]
    nt = seq // tm
    row = lambda i: (i, 0)
    fixed = lambda i: (0, 0)
    tab = lambda i: (i % nt, 0)
    kv_w = 2 * N_KV_HEADS * HD
    out_shape = (
        jax.ShapeDtypeStruct((t, PA_W), F32),
        jax.ShapeDtypeStruct((t, LANE), F32),
        jax.ShapeDtypeStruct((t, N_Q_HEADS * HD), BF16),
        jax.ShapeDtypeStruct((t, kv_w), BF16),
        jax.ShapeDtypeStruct((t, kv_w), BF16),
        jax.ShapeDtypeStruct((t, N_Q_HEADS * HD), BF16),
        jax.ShapeDtypeStruct((t, kv_w), BF16),
        jax.ShapeDtypeStruct((t, kv_w), BF16),
    )
    in_specs = [
        pl.BlockSpec((tm, D_MODEL), row),
        pl.BlockSpec((1, D_MODEL), fixed),
        pl.BlockSpec((D_MODEL, PROJ_W), fixed),
        pl.BlockSpec((1, QK_W), fixed),
        pl.BlockSpec((1, QK_W), fixed),
    ] + [pl.BlockSpec((tm, LANE), tab)] * 4
    out_specs = tuple(pl.BlockSpec((tm, s.shape[1]), row) for s in out_shape)
    return pl.pallas_call(
        _inproj_kernel, out_shape=out_shape, grid=(t // tm,),
        in_specs=in_specs, out_specs=out_specs,
        compiler_params=_params(("parallel",)), name="inproj",
    )(x2, g1, w_re, gain_b, gain_c, *tabs)


def _rope_tables(seq):
    def tables(pos, dim):
        inv = ROPE_THETA ** (-jnp.arange(0, dim, 2, dtype=F32) / dim)
        ang = pos.astype(F32)[:, None] * inv[None, :]
        cos = jnp.concatenate([jnp.cos(ang), jnp.cos(ang)], axis=-1)
        sin = jnp.concatenate([-jnp.sin(ang), jnp.sin(ang)], axis=-1)
        return cos, sin
    pos = jnp.arange(seq)
    cos1, sin1 = tables(pos, HD)
    cosr, sinr = tables(pos // GRID_W, HD // 2)
    cosc, sinc = tables(pos % GRID_W, HD // 2)
    cos2 = jnp.concatenate([cosr, cosc], axis=-1)
    sin2 = jnp.concatenate([sinr, sinc], axis=-1)
    two = lambda a: jnp.concatenate([a, a], axis=-1)
    return two(cos1), two(sin1), two(cos2), two(sin2)


NEG_BIG = -1e30


def _attn_kernel(sink_ref, q_ref, k_ref, v_ref, o_ref, *, tq, tk, seq, window):
    qi = pl.program_id(1)
    q0 = qi * tq
    if window:
        lo = jnp.maximum(q0 - WINDOW, 0) // tk
        hi = (jnp.minimum(q0 + tq + WINDOW, seq) + tk - 1) // tk
    else:
        lo, hi = 0, seq // tk
    lane = lax.broadcasted_iota(jnp.int32, (tq, LANE), 1)
    row2 = lax.broadcasted_iota(jnp.int32, (2 * tq, 1), 0)
    if window:
        qpos = q0 + row2 % tq
    groups = range(N_KV_HEADS)
    cols = [slice(g * LANE, (g + 1) * LANE) for g in groups]
    q2s, init = [], []
    for g in groups:
        qg = q_ref[:, cols[g]]
        zero = jnp.zeros_like(qg)
        q2s.append(jnp.concatenate([jnp.where(lane < HD, qg, zero),
                                    jnp.where(lane >= HD, qg, zero)], axis=0))
        if window:
            m0 = jnp.where(row2 < tq, sink_ref[2 * g], sink_ref[2 * g + 1])
            l0 = jnp.ones((2 * tq, 1), F32)
        else:
            m0 = jnp.full((2 * tq, 1), NEG_BIG, F32)
            l0 = jnp.zeros((2 * tq, 1), F32)
        init.append((m0, l0, jnp.zeros((2 * tq, LANE), F32)))

    def body(c, carry):
        k0 = pl.multiple_of(c * tk, tk)
        out = []
        for g in groups:
            m, l, acc = carry[g]
            kc = k_ref[pl.ds(k0, tk), cols[g]]
            vc = v_ref[pl.ds(k0, tk), cols[g]]
            s = lax.dot_general(q2s[g], kc, (((1,), (1,)), ((), ())),
                                preferred_element_type=F32)
            if window:
                kpos = k0 + lax.broadcasted_iota(jnp.int32, (1, tk), 1)
                s = jnp.where(jnp.abs(qpos - kpos) <= WINDOW, s, NEG_BIG)
            m_new = jnp.maximum(m, jnp.max(s, axis=1, keepdims=True))
            alpha = jnp.exp2(m - m_new)
            p = jnp.exp2(s - m_new)
            l = alpha * l + jnp.sum(p, axis=1, keepdims=True)
            acc = alpha * acc + jnp.dot(p.astype(BF16), vc, preferred_element_type=F32)
            out.append((m_new, l, acc))
        return tuple(out)

    res = lax.fori_loop(lo, hi, body, tuple(init))
    for g in groups:
        _, l, acc = res[g]
        o = acc / l
        o_ref[:, cols[g]] = jnp.where(lane < HD, o[:tq], o[tq:]).astype(o_ref.dtype)


def _attention(q, kd, vd, sinks, batch, seq, tq, tk, window):
    t = q.shape[0]
    nq = seq // tq
    width = N_Q_HEADS * HD
    kern = functools.partial(_attn_kernel, tq=tq, tk=tk, seq=seq, window=window)
    return pl.pallas_call(
        kern, out_shape=jax.ShapeDtypeStruct((t, width), BF16), grid=(batch, nq),
        in_specs=[pl.BlockSpec(memory_space=pltpu.SMEM),
                  pl.BlockSpec((tq, width), lambda b, i: (b * nq + i, 0)),
                  pl.BlockSpec((seq, width), lambda b, i: (b, 0)),
                  pl.BlockSpec((seq, width), lambda b, i: (b, 0))],
        out_specs=pl.BlockSpec((tq, width), lambda b, i: (b * nq + i, 0)),
        compiler_params=_params(("parallel", "parallel")),
        name="window_attention" if window else "dense_attention",
    )(sinks, q, kd, vd)


N_INST = 2 * A_HEADS
HALO = 8
HI = lax.Precision.HIGHEST


def _dot_hi(a, b):
    return jnp.dot(a, b, preferred_element_type=F32, precision=HI)


def _dot_bf(a, b):
    return jnp.dot(a.astype(BF16), b.astype(BF16), preferred_element_type=F32)


def _unit_tri_inverse(low):
    n = low.shape[0]
    r = lax.broadcasted_iota(jnp.int32, (n, n), 0)
    c = lax.broadcasted_iota(jnp.int32, (n, n), 1)
    neg = -low
    x = jnp.where(r == c, 1.0, 0.0) + neg
    p = neg
    for _ in range(int(math.log2(n)) - 1):
        p = _dot_bf(p, p)
        x = x + _dot_bf(x, p)
    return x


def _delta_prep_kernel(prev_ref, cur_ref, next_ref, gb_ref, cw_ref, aneg_ref, dtb_ref,
                       u_ref, w_ref, qg_ref, kd_ref, a_ref, dl_ref,
                       qs, ks, vs, gs, bs, *, tt, tiles_per_seq):
    i = pl.program_id(0) % tiles_per_seq
    prev = jnp.where(i == 0, 0.0, prev_ref[...])
    nxt = jnp.where(i == tiles_per_seq - 1, 0.0, next_ref[...])
    ext = jnp.concatenate([prev, cur_ref[...], nxt], axis=0)
    y = jnp.zeros((tt, QKV_A), F32)
    for j in range(A_CONV):
        s = HALO - A_CONV // 2 + j
        y = y + ext[s:s + tt, :] * cw_ref[j:j + 1, :]
    y = y / (1.0 + jnp.exp(-y))
    hw = A_HEADS * A_DK
    for h in range(A_HEADS):
        for src, dst, scale in ((0, qs, A_DK ** -0.5), (hw, ks, 1.0)):
            xh = y[:, src + h * A_DK: src + (h + 1) * A_DK]
            ss = jnp.sum(xh * xh, axis=1, keepdims=True)
            dst[:, h * A_DK:(h + 1) * A_DK] = xh * (lax.rsqrt(ss + EPS) * scale)
    vs[...] = y[:, 2 * hw:]
    gbv = gb_ref[...]
    bs[...] = 1.0 / (1.0 + jnp.exp(-gbv))
    xg = gbv + dtb_ref[...]
    softplus = jnp.maximum(xg, 0.0) + jnp.log1p(jnp.exp(-jnp.abs(xg)))
    gs[...] = aneg_ref[...] * softplus

    n = A_CHUNK
    r = lax.broadcasted_iota(jnp.int32, (n, n), 0)
    c = lax.broadcasted_iota(jnp.int32, (n, n), 1)
    lo_incl = jnp.where(c <= r, 1.0, 0.0)
    up_incl = jnp.where(c >= r, 1.0, 0.0)

    def chunk(ci, carry):
        r0 = pl.multiple_of(ci * n, n)
        rows = pl.ds(r0, n)
        g8 = gs[rows, :]
        b8 = bs[rows, :]
        gtot = jnp.sum(g8, axis=0, keepdims=True)
        dl_ref[pl.ds(ci, 1), :] = jnp.exp(gtot)
        for d in range(2):
            tri = lo_incl if d == 0 else up_incl
            keep = (c <= r) if d == 0 else (c >= r)
            strict = (c < r) if d == 0 else (c > r)
            gc_all = _dot_hi(tri, g8)
            gc_row = lax.dot_general(g8, tri, (((0,), (1,)), ((), ())),
                                     preferred_element_type=F32, precision=HI)
            for h in range(A_HEADS):
                inst = d * A_HEADS + h
                gl = N_INST + inst
                gc = gc_all[:, gl:gl + 1]
                diff = gc - gc_row[gl:gl + 1, :]
                decay = jnp.where(keep, jnp.exp(jnp.where(keep, diff, 0.0)), 0.0)
                eg = jnp.exp(gc)
                beta = b8[:, inst:inst + 1]
                hs = slice(h * A_DK, (h + 1) * A_DK)
                qh = qs[rows, hs]
                kh = ks[rows, hs]
                vh = vs[rows, hs]
                kb = kh * beta
                both = lax.dot_general(jnp.concatenate([kb, qh], axis=0).astype(BF16),
                                       kh.astype(BF16), (((1,), (1,)), ((), ())),
                                       preferred_element_type=F32)
                tm = _unit_tri_inverse(jnp.where(strict, both[:n] * decay, 0.0))
                uw = _dot_bf(tm, jnp.concatenate([vh * beta, kb * eg], axis=1))
                o128 = slice(inst * A_DK, (inst + 1) * A_DK)
                u_ref[rows, o128] = uw[:, :A_DV]
                w_ref[rows, o128] = uw[:, A_DV:]
                qg_ref[rows, o128] = qh * eg
                kd_ref[rows, o128] = kh * jnp.exp(gtot[:, gl:gl + 1] - gc)
                a_ref[rows, inst * n:(inst + 1) * n] = both[n:] * decay
        return carry

    lax.fori_loop(0, tt // n, chunk, 0)


def _delta_prep(pa, gb, conv_w8, aneg, dtb, seq, tt):
    t = pa.shape[0]
    tps = seq // tt
    hb = tt // HALO
    last = t // HALO - 1
    wide = N_INST * A_DK
    kern = functools.partial(_delta_prep_kernel, tt=tt, tiles_per_seq=tps)
    big = jax.ShapeDtypeStruct((t, wide), F32)
    out_shape = (big, big, big, big,
                 jax.ShapeDtypeStruct((t, N_INST * A_CHUNK), F32),
                 jax.ShapeDtypeStruct((t // A_CHUNK, LANE), F32))
    row = lambda i: (i, 0)
    fixed = lambda i: (0, 0)
    return pl.pallas_call(
        kern, out_shape=out_shape, grid=(t // tt,),
        in_specs=[pl.BlockSpec((HALO, QKV_A), lambda i: (jnp.maximum(i * hb - 1, 0), 0)),
                  pl.BlockSpec((tt, QKV_A), row),
                  pl.BlockSpec((HALO, QKV_A), lambda i: (jnp.minimum((i + 1) * hb, last), 0)),
                  pl.BlockSpec((tt, LANE), row),
                  pl.BlockSpec((HALO, QKV_A), fixed),
                  pl.BlockSpec((1, LANE), fixed),
                  pl.BlockSpec((1, LANE), fixed)],
        out_specs=(pl.BlockSpec((tt, wide), row),) * 4
        + (pl.BlockSpec((tt, N_INST * A_CHUNK), row),
           pl.BlockSpec((tt // A_CHUNK, LANE), row)),
        scratch_shapes=[pltpu.VMEM((tt, A_HEADS * A_DK), F32)] * 3
        + [pltpu.VMEM((tt, LANE), F32)] * 2,
        compiler_params=_params(("parallel",)), name="delta_prep",
    )(pa, pa, pa, gb, conv_w8, aneg, dtb)


def _delta_scan_kernel(uf, wf, qf, kf, af, dlf, ub, wb, qb, kb, ab, dlb,
                       of_ref, ob_ref, state, *, cb):
    @pl.when(pl.program_id(1) == 0)
    def _():
        state[...] = jnp.zeros_like(state)

    n = A_CHUNK
    dirs = ((uf, wf, qf, kf, af, dlf, of_ref), (ub, wb, qb, kb, ab, dlb, ob_ref))

    def chunk(ci, carry):
        for d, (u, w, q, k, a, dl, o) in enumerate(dirs):
            cc = ci if d == 0 else cb - 1 - ci
            rows = pl.ds(pl.multiple_of(cc * n, n), n)
            dlrow = dl[pl.ds(cc, 1), :]
            for h in range(A_HEADS):
                inst = d * A_HEADS + h
                hs = slice(h * A_DK, (h + 1) * A_DK)
                st = state[inst]
                stb = st.astype(BF16)
                v_new = u[rows, hs] - jnp.dot(w[rows, hs].astype(BF16), stb,
                                              preferred_element_type=F32)
                vb16 = v_new.astype(BF16)
                o[rows, hs] = (jnp.dot(q[rows, hs].astype(BF16), stb, preferred_element_type=F32)
                               + jnp.dot(a[rows, h * n:(h + 1) * n].astype(BF16), vb16,
                                         preferred_element_type=F32))
                upd = lax.dot_general(k[rows, hs].astype(BF16), vb16, (((0,), (0,)), ((), ())),
                                      preferred_element_type=F32)
                gl = N_INST + inst
                state[inst] = st * dlrow[:, gl:gl + 1] + upd
        return carry

    lax.fori_loop(0, cb, chunk, 0)


def _delta_scan(u, w, qg, kd, a, dl, batch, seq, cb):
    t = u.shape[0]
    tt = cb * A_CHUNK
    nb = seq // tt
    hw = A_HEADS * A_DV
    fwd = lambda b, i: (b * nb + i, 0)
    bwd = lambda b, i: (b * nb + nb - 1 - i, 1)
    bwd0 = lambda b, i: (b * nb + nb - 1 - i, 0)
    big_f = pl.BlockSpec((tt, hw), fwd)
    big_b = pl.BlockSpec((tt, hw), bwd)
    aw = A_HEADS * A_CHUNK
    kern = functools.partial(_delta_scan_kernel, cb=cb)
    out = jax.ShapeDtypeStruct((t, hw), F32)
    return pl.pallas_call(
        kern, out_shape=(out, out), grid=(batch, nb),
        in_specs=[big_f] * 4 + [pl.BlockSpec((tt, aw), fwd), pl.BlockSpec((cb, LANE), fwd)]
        + [big_b] * 4 + [pl.BlockSpec((tt, aw), bwd), pl.BlockSpec((cb, LANE), bwd0)],
        out_specs=(pl.BlockSpec((tt, hw), fwd), pl.BlockSpec((tt, hw), bwd0)),
        scratch_shapes=[pltpu.VMEM((N_INST, A_DK, A_DV), F32)],
        compiler_params=_params(("parallel", "arbitrary")), name="delta_scan",
    )(u, w, qg, kd, a, dl, u, w, qg, kd, a, dl)


def _outproj_kernel(of_ref, ob_ref, z_ref, wb_ref, dc_ref, ga_ref, wo_ref, x_ref, g2_ref, wr_ref,
                    x1_ref, h2_ref, aff_ref):
    o = of_ref[...] + ob_ref[...]
    z = z_ref[...]
    parts = []
    for h in range(A_HEADS):
        hs = slice(h * A_DV, (h + 1) * A_DV)
        oh = o[:, hs]
        ms = jnp.mean(oh * oh, axis=1, keepdims=True)
        zh = z[:, hs]
        parts.append((oh * lax.rsqrt(ms + EPS) * ga_ref[...] * (zh / (1.0 + jnp.exp(-zh)))
                      ).astype(BF16))
    mixed = jnp.concatenate(parts + [wb_ref[...], dc_ref[...]], axis=1)
    x1 = x_ref[...] + jnp.dot(mixed, wo_ref[...], preferred_element_type=F32)
    x1_ref[...] = x1
    ms = jnp.mean(x1 * x1, axis=1, keepdims=True)
    h2 = x1 * lax.rsqrt(ms + EPS) * g2_ref[...]
    h2_ref[...] = h2.astype(BF16)
    logits = lax.dot_general(wr_ref[...], h2, (((1,), (1,)), ((), ())),
                             preferred_element_type=F32, precision=HI)
    m = jnp.max(logits, axis=0, keepdims=True)
    p = jnp.exp(logits - m)
    aff_ref[0] = p / jnp.sum(p, axis=0, keepdims=True)


def _outproj(o_f, o_b, pa, wnd, dns, ga, w_out, x2, g2, wr_t, batch, seq, tm):
    t = x2.shape[0]
    nt = seq // tm
    row = lambda i: (i, 0)
    fixed = lambda i: (0, 0)
    hw = A_HEADS * A_DV
    aw = N_Q_HEADS * HD
    return pl.pallas_call(
        _outproj_kernel,
        out_shape=(jax.ShapeDtypeStruct((t, D_MODEL), F32),
                   jax.ShapeDtypeStruct((t, D_MODEL), BF16),
                   jax.ShapeDtypeStruct((batch, N_EXPERTS, seq), F32)),
        grid=(t // tm,),
        in_specs=[pl.BlockSpec((tm, hw), row), pl.BlockSpec((tm, hw), row),
                  pl.BlockSpec((tm, hw), lambda i: (i, QKV_A // hw)),
                  pl.BlockSpec((tm, aw), row), pl.BlockSpec((tm, aw), row),
                  pl.BlockSpec((1, A_DV), fixed),
                  pl.BlockSpec((D_MODEL, D_MODEL), fixed),
                  pl.BlockSpec((tm, D_MODEL), row),
                  pl.BlockSpec((1, D_MODEL), fixed),
                  pl.BlockSpec((N_EXPERTS, D_MODEL), fixed)],
        out_specs=(pl.BlockSpec((tm, D_MODEL), row), pl.BlockSpec((tm, D_MODEL), row),
                   pl.BlockSpec((1, N_EXPERTS, tm), lambda i: (i // nt, 0, i % nt))),
        compiler_params=_params(("parallel",)), name="outproj_router",
    )(o_f, o_b, pa, wnd, dns, ga, w_out, x2, g2, wr_t)


def _excl_cumsum(mask01, tri, ones, blk_strict):
    mb = mask01.astype(BF16)
    incl = jnp.dot(mb, tri, preferred_element_type=F32)
    tot = jnp.dot(mb, ones, preferred_element_type=F32)
    rowoff = jnp.dot(blk_strict, tot.astype(BF16), preferred_element_type=F32)
    return rowoff + incl - mask01, rowoff


def _select_kernel(aff_ref, pos_ref, gate_ref, off_ref, *, nr, cap):
    aff = aff_ref[0]
    rows = aff.shape[0]
    aff3 = aff.reshape(N_EXPERTS, nr, LANE)

    def count(mask):
        c = jnp.sum(jnp.where(mask, 1.0, 0.0), axis=2, keepdims=True)
        return jnp.sum(c, axis=1, keepdims=True)

    def as_float(word):
        return lax.bitcast_convert_type(word, F32)

    def search(i, thr):
        cand = thr | lax.shift_left(jnp.int32(1), 30 - i)
        return jnp.where(count(aff3 >= as_float(cand)) >= cap, cand, thr)

    thr = lax.fori_loop(0, 31, search, jnp.zeros((N_EXPERTS, 1, 1), jnp.int32))
    gt = aff3 >= as_float(thr + 1)
    eq = (aff3 >= as_float(thr)) & jnp.logical_not(gt)
    need = cap - count(gt)

    r = lax.broadcasted_iota(jnp.int32, (LANE, LANE), 0)
    c = lax.broadcasted_iota(jnp.int32, (LANE, LANE), 1)
    tri = jnp.where(r <= c, 1.0, 0.0).astype(BF16)
    ones = jnp.ones((LANE, LANE), BF16)
    rr = lax.broadcasted_iota(jnp.int32, (rows, rows), 0)
    cc = lax.broadcasted_iota(jnp.int32, (rows, rows), 1)
    blk_strict = jnp.where((rr // nr == cc // nr) & (cc < rr), 1.0, 0.0).astype(BF16)

    eq01 = jnp.where(eq, 1.0, 0.0).reshape(rows, LANE)
    eq_rank, _ = _excl_cumsum(eq01, tri, ones, blk_strict)
    sel = gt | (eq & (eq_rank.reshape(N_EXPERTS, nr, LANE) < need))
    sel01 = jnp.where(sel, 1.0, 0.0).reshape(rows, LANE)
    pos, rowoff = _excl_cumsum(sel01, tri, ones, blk_strict)
    pos_ref[0] = jnp.where(sel01 > 0.0, pos, -1.0).astype(jnp.int32)
    gate_ref[0] = jnp.where(sel01 > 0.0, aff, 0.0)
    off_ref[0] = rowoff.astype(jnp.int32)


def _select(aff_t, batch, seq):
    nr = seq // LANE
    rows = N_EXPERTS * nr
    cap = CAP_FACTOR * seq // N_EXPERTS
    aff3 = aff_t.reshape(batch, rows, LANE)
    spec = pl.BlockSpec((1, rows, LANE), lambda b: (b, 0, 0))
    kern = functools.partial(_select_kernel, nr=nr, cap=cap)
    return pl.pallas_call(
        kern,
        out_shape=(jax.ShapeDtypeStruct((batch, rows, LANE), jnp.int32),
                   jax.ShapeDtypeStruct((batch, rows, LANE), F32),
                   jax.ShapeDtypeStruct((batch, rows, LANE), jnp.int32)),
        grid=(batch,), in_specs=[spec], out_specs=(spec, spec, spec),
        compiler_params=_params(("parallel",)), name="expert_select",
    )(aff3)


TOK_BLK = 2 * LANE
WIN = 64


def _tok_lanes(ref, b0, row):
    two = ref[b0, pl.ds(row, 2), :]
    return jnp.concatenate([two[0:1], two[1:2]], axis=1)


def _moe_up_kernel(boff_ref, h2_ref, pos_ref, gate_ref, wg_ref, wu_ref, hid_ref, xs, gr,
                   *, nblk, cap, ftile):
    b = pl.program_id(0)
    e = pl.program_id(1)
    base = (b * N_EXPERTS + e) * (nblk + 1)
    xs[...] = jnp.zeros_like(xs)
    gr[...] = jnp.zeros_like(gr)
    riota = lax.broadcasted_iota(jnp.int32, (WIN, TOK_BLK), 0)

    def block(j, carry):
        off = boff_ref[base + j]
        cnt = boff_ref[base + j + 1] - off
        r0 = (off // 8) * 8
        nwin = jnp.where(cnt > 0, (off + cnt - r0 + WIN - 1) // WIN, 0)
        prow = _tok_lanes(pos_ref, 0, 2 * j)
        grow = _tok_lanes(gate_ref, 0, 2 * j)
        toks = h2_ref[0, pl.ds(pl.multiple_of(j * TOK_BLK, TOK_BLK), TOK_BLK), :]

        def window(k, c2):
            rb = pl.multiple_of(r0 + k * WIN, 8)
            hit = (riota + rb) == prow
            onehot = jnp.where(hit, 1.0, 0.0).astype(BF16)
            xs[pl.ds(rb, WIN), :] += jnp.dot(onehot, toks, preferred_element_type=F32)
            g = jnp.sum(jnp.where(hit, grow, 0.0), axis=1, keepdims=True)
            gr[pl.ds(rb, WIN), :] += jnp.broadcast_to(g, (WIN, LANE))
            return c2

        lax.fori_loop(0, nwin, window, 0)
        return carry

    lax.fori_loop(0, nblk, block, 0)
    xsb = xs[0:cap, :].astype(BF16)
    gate = jnp.concatenate([gr[0:cap, :]] * (ftile // LANE), axis=1)
    for f in range(D_EXPERT // ftile):
        fs = slice(f * ftile, (f + 1) * ftile)
        g = jnp.dot(xsb, wg_ref[0, :, fs], preferred_element_type=F32)
        u = jnp.dot(xsb, wu_ref[0, :, fs], preferred_element_type=F32)
        hid_ref[0, 0, :, fs] = (g / (1.0 + jnp.exp(-g)) * u * gate).astype(BF16)


def _moe_up(boff, h2, pos, gate, wg, wu, batch, seq):
    nr = seq // LANE
    nblk = seq // TOK_BLK
    cap = CAP_FACTOR * seq // N_EXPERTS
    kern = functools.partial(_moe_up_kernel, nblk=nblk, cap=cap, ftile=512)
    grid_spec = pltpu.PrefetchScalarGridSpec(
        num_scalar_prefetch=1, grid=(batch, N_EXPERTS),
        in_specs=[pl.BlockSpec((1, seq, D_MODEL), lambda b, e, s: (b, 0, 0)),
                  pl.BlockSpec((1, nr, LANE), lambda b, e, s: (b, e, 0)),
                  pl.BlockSpec((1, nr, LANE), lambda b, e, s: (b, e, 0)),
                  pl.BlockSpec((1, D_MODEL, D_EXPERT), lambda b, e, s: (e, 0, 0)),
                  pl.BlockSpec((1, D_MODEL, D_EXPERT), lambda b, e, s: (e, 0, 0))],
        out_specs=pl.BlockSpec((1, 1, cap, D_EXPERT), lambda b, e, s: (b, e, 0, 0)),
        scratch_shapes=[pltpu.VMEM((cap + WIN, D_MODEL), F32),
                        pltpu.VMEM((cap + WIN, LANE), F32)])
    return pl.pallas_call(
        kern, out_shape=jax.ShapeDtypeStruct((batch, N_EXPERTS, cap, D_EXPERT), BF16),
        grid_spec=grid_spec,
        compiler_params=_params(("arbitrary", "arbitrary")), name="moe_gather_up",
    )(boff, h2, pos, gate, wg, wu)


GROUP = 4


def _moe_down_kernel(boff_ref, hid_ref, wd_ref, pos_ref, x1_ref, out_ref, ys,
                     *, nblk, nr, cap):
    b = pl.program_id(0)
    e = pl.program_id(2)

    @pl.when(e == 0)
    def _():
        ys[:, cap:, :] = jnp.zeros((N_EXPERTS, WIN, ys.shape[2]), ys.dtype)

    ys[e, 0:cap, :] = jnp.dot(hid_ref[0, 0], wd_ref[0], preferred_element_type=F32
                              ).astype(ys.dtype)

    @pl.when(e == N_EXPERTS - 1)
    def _():
        riota = lax.broadcasted_iota(jnp.int32, (WIN, TOK_BLK), 0)

        def block(j, carry):
            rows = pl.ds(pl.multiple_of(j * TOK_BLK, TOK_BLK), TOK_BLK)
            acc = x1_ref[0, rows, :]
            for g0 in range(0, N_EXPERTS, GROUP):
                starts, prows, npass = [], [], 0
                for e2 in range(g0, g0 + GROUP):
                    base = (b * N_EXPERTS + e2) * (nblk + 1)
                    off = boff_ref[base + j]
                    cnt = boff_ref[base + j + 1] - off
                    r0 = (off // 16) * 16
                    nwin = jnp.where(cnt > 0, (off + cnt - r0 + WIN - 1) // WIN, 0)
                    npass = jnp.maximum(npass, nwin)
                    starts.append(r0)
                    prows.append(_tok_lanes(pos_ref, 0, e2 * nr + 2 * j))

                def one_pass(k, a):
                    hots, vals = [], []
                    for i2, e2 in enumerate(range(g0, g0 + GROUP)):
                        rb = pl.multiple_of(jnp.minimum(starts[i2] + k * WIN, cap), 16)
                        hots.append(jnp.where((riota + rb) == prows[i2], 1.0, 0.0).astype(BF16))
                        vals.append(ys[e2, pl.ds(rb, WIN), :])
                    return a + lax.dot_general(
                        jnp.concatenate(hots, axis=0), jnp.concatenate(vals, axis=0),
                        (((0,), (0,)), ((), ())), preferred_element_type=F32)

                acc = lax.fori_loop(0, npass, one_pass, acc)
            out_ref[0, rows, :] = acc
            return carry

        lax.fori_loop(0, nblk, block, 0)


def _moe_down(boff, hid, wd, pos, x1, batch, seq, td):
    nr = seq // LANE
    nblk = seq // TOK_BLK
    cap = CAP_FACTOR * seq // N_EXPERTS
    kern = functools.partial(_moe_down_kernel, nblk=nblk, nr=nr, cap=cap)
    grid_spec = pltpu.PrefetchScalarGridSpec(
        num_scalar_prefetch=1, grid=(batch, D_MODEL // td, N_EXPERTS),
        in_specs=[pl.BlockSpec((1, 1, cap, D_EXPERT), lambda b, d, e, s: (b, e, 0, 0)),
                  pl.BlockSpec((1, D_EXPERT, td), lambda b, d, e, s: (e, 0, d)),
                  pl.BlockSpec((1, N_EXPERTS * nr, LANE), lambda b, d, e, s: (b, 0, 0)),
                  pl.BlockSpec((1, seq, td), lambda b, d, e, s: (b, 0, d))],
        out_specs=pl.BlockSpec((1, seq, td), lambda b, d, e, s: (b, 0, d)),
        scratch_shapes=[pltpu.VMEM((N_EXPERTS, cap + WIN, td), BF16)])
    return pl.pallas_call(
        kern, out_shape=jax.ShapeDtypeStruct((batch, seq, D_MODEL), F32),
        grid_spec=grid_spec,
        compiler_params=_params(("arbitrary", "arbitrary", "arbitrary")),
        name="moe_down_scatter",
    )(boff, hid, wd, pos, x1)


def _block_offsets(rowoff, batch, seq):
    nr = seq // LANE
    cap = CAP_FACTOR * seq // N_EXPERTS
    per_row = rowoff[:, :, 0].reshape(batch, N_EXPERTS, nr)
    step = TOK_BLK // LANE
    offs = jnp.concatenate([per_row[:, :, ::step],
                            jnp.full((batch, N_EXPERTS, 1), cap, jnp.int32)], axis=2)
    return offs.reshape(-1)


def _delta_params(conv_w, a_log, dt_bias):
    cw8 = jnp.concatenate([conv_w.astype(F32),
                           jnp.zeros((HALO - A_CONV, conv_w.shape[1]), F32)], axis=0)
    pad_l = jnp.zeros((N_INST,), F32)
    pad_r = jnp.zeros((LANE - 2 * N_INST,), F32)
    aneg = jnp.concatenate([pad_l, -jnp.exp(a_log.astype(F32)).reshape(-1), pad_r])[None, :]
    dtb = jnp.concatenate([pad_l, dt_bias.astype(F32).reshape(-1), pad_r])[None, :]
    return cw8, aneg, dtb


def _reorder_w_in(w):
    d = w.shape[0]
    gates = w[:, PA_W:PA_W + 4 * A_HEADS]
    rest = w[:, PA_W + 4 * A_HEADS:]
    pad = jnp.zeros((d, PROJ_W - OFF_G - 4 * A_HEADS), w.dtype)
    return jnp.concatenate([w[:, :PA_W], rest, gates, pad], axis=1)


def _qk_gains(gq, gk):
    scale = (HD ** -0.5) * math.log2(math.e)
    return jnp.concatenate([jnp.tile(gq.astype(F32) * scale, N_Q_HEADS),
                            jnp.tile(gk.astype(F32), N_KV_HEADS)])[None, :]


def _tiles(seq):
    tm = min(512, seq)
    return dict(tm=tm, tq=min(256, seq), tk_dense=min(512, seq), tk_win=LANE,
                tt=tm, cb=tm // A_CHUNK, td=256)


def _layer(x, p, l, tabs, batch, seq):
    ts = _tiles(seq)
    x2 = x.reshape(batch * seq, D_MODEL)
    w_re = _reorder_w_in(p["w_in"][l]).astype(BF16)
    gain_b = _qk_gains(p["b_qnorm_g"][l], p["b_knorm_g"][l])
    gain_c = _qk_gains(p["c_qnorm_g"][l], p["c_knorm_g"][l])
    pa, gb, qb, kb, vb, qc, kc, vc = _inproj(
        x2, p["norm1_g"][l].astype(F32)[None, :], w_re, gain_b, gain_c, tabs, seq, ts["tm"])

    cw8, aneg, dtb = _delta_params(p["conv_w"][l], p["a_log"][l], p["dt_bias"][l])
    u, w, qg, kd, a, dl = _delta_prep(pa, gb, cw8, aneg, dtb, seq, ts["tt"])
    o_f, o_b = _delta_scan(u, w, qg, kd, a, dl, batch, seq, ts["cb"])

    sinks = p["b_sink"][l].astype(F32) * math.log2(math.e)
    wnd = _attention(qb, kb, vb, sinks, batch, seq, ts["tq"], ts["tk_win"], True)
    dns = _attention(qc, kc, vc, sinks, batch, seq, ts["tq"], ts["tk_dense"], False)

    x1, h2, aff_t = _outproj(
        o_f, o_b, pa, wnd, dns, p["a_norm_g"][l].astype(F32)[None, :],
        p["w_out"][l].astype(BF16), x2, p["norm2_g"][l].astype(F32)[None, :],
        p["w_router"][l].astype(F32).T, batch, seq, ts["tm"])

    pos, gate, rowoff = _select(aff_t, batch, seq)
    boff = _block_offsets(rowoff, batch, seq)
    hid = _moe_up(boff, h2.reshape(batch, seq, D_MODEL), pos, gate,
                  p["w_gate"][l].astype(BF16), p["w_up"][l].astype(BF16), batch, seq)
    return _moe_down(boff, hid, p["w_down"][l].astype(BF16), pos,
                     x1.reshape(batch, seq, D_MODEL), batch, seq, ts["td"])


def kernel(x, norm1_g, w_in, conv_w, a_log, dt_bias, a_norm_g, b_qnorm_g, b_knorm_g, b_sink,
           c_qnorm_g, c_knorm_g, w_out, norm2_g, w_router, w_gate, w_up, w_down):
    batch, seq, _ = x.shape
    p = dict(norm1_g=norm1_g, w_in=w_in, conv_w=conv_w, a_log=a_log, dt_bias=dt_bias,
             a_norm_g=a_norm_g, b_qnorm_g=b_qnorm_g, b_knorm_g=b_knorm_g, b_sink=b_sink,
             c_qnorm_g=c_qnorm_g, c_knorm_g=c_knorm_g, w_out=w_out, norm2_g=norm2_g,
             w_router=w_router, w_gate=w_gate, w_up=w_up, w_down=w_down)
    tabs = _rope_tables(seq)
    for l in range(w_in.shape[0]):
        x = _layer(x, p, l, tabs, batch, seq)
    return x
```

```python
import functools
import math

import jax
import jax.numpy as jnp
from jax import lax
from jax.experimental import pallas as pl
from jax.experimental.pallas import tpu as pltpu

F32 = jnp.float32
BF16 = jnp.bfloat16

D_MODEL = 1024
GRID_W = 64
EPS = 1e-6
ROPE_THETA = 10000.0
A_HEADS = 4
A_DK = 128
A_DV = 128
A_CONV = 5
A_CHUNK = 64
HD = 64
N_Q_HEADS = 4
N_KV_HEADS = 2
WINDOW = 128
N_EXPERTS = 16
CAP_FACTOR = 2
D_EXPERT = 1024

QKV_A = 3 * A_HEADS * A_DK
PA_W = QKV_A + A_HEADS * A_DV
ATT_W = (N_Q_HEADS + 2 * N_KV_HEADS) * HD
QK_W = (N_Q_HEADS + N_KV_HEADS) * HD
OFF_B = PA_W
OFF_C = PA_W + ATT_W
OFF_G = PA_W + 2 * ATT_W
LANE = 128
PROJ_W = OFF_G + LANE

VMEM_LIMIT = 56 * 1024 * 1024


def _params(sem, vmem=None):
    return pltpu.CompilerParams(dimension_semantics=sem,
                                vmem_limit_bytes=vmem or VMEM_LIMIT)


def _group_sumsq(v, width):
    n = v.shape[-1]
    r = lax.broadcasted_iota(jnp.int32, (n, n), 0) // width
    c = lax.broadcasted_iota(jnp.int32, (n, n), 1) // width
    ones = jnp.where(r == c, 1.0, 0.0).astype(BF16)
    sq = v * v
    hi = sq.astype(BF16)
    lo = (sq - hi.astype(F32)).astype(BF16)
    return (jnp.dot(hi, ones, preferred_element_type=F32)
            + jnp.dot(lo, ones, preferred_element_type=F32))


def _dup_halves(v):
    lane = lax.broadcasted_iota(jnp.int32, v.shape, 1)
    sw = pltpu.roll(v, 64, 1)
    lo = lane < 64
    return jnp.concatenate([jnp.where(lo, v, sw), jnp.where(lo, sw, v)], axis=1)


def _with_ones(v):
    lane = lax.broadcasted_iota(jnp.int32, v.shape, 1)
    lo = lane < 64
    return jnp.concatenate([jnp.where(lo, v, 1.0),
                            jnp.where(lo, pltpu.roll(v, 64, 1), 1.0)], axis=1)


def _inproj_kernel(x_ref, g1_ref, w_ref, gain_b_ref, gain_c_ref,
                   cosb_ref, sinb_ref, cosc_ref, sinc_ref,
                   pa_ref, gb_ref, qb_ref, kb_ref, vb_ref, qc_ref, kc_ref, vc_ref):
    x = x_ref[...]
    ms = jnp.mean(x * x, axis=-1, keepdims=True)
    h = (x * lax.rsqrt(ms + EPS) * g1_ref[...]).astype(BF16)
    pa_ref[...] = jnp.dot(h, w_ref[:, 0:PA_W], preferred_element_type=F32)
    gb_ref[...] = jnp.dot(h, w_ref[:, OFF_G:PROJ_W], preferred_element_type=F32)

    def softmax_mixer(off, gain_ref, cos_ref, sin_ref, half, q_ref, k_ref, v_ref):
        y = jnp.dot(h, w_ref[:, off:off + ATT_W], preferred_element_type=F32)
        qk = y[:, :QK_W]
        ss = _group_sumsq(qk, HD)
        n = qk * lax.rsqrt(ss * (1.0 / HD) + EPS) * gain_ref[...]
        cos = jnp.concatenate([cos_ref[...]] * (QK_W // LANE), axis=1)
        sin = jnp.concatenate([sin_ref[...]] * (QK_W // LANE), axis=1)
        lane = lax.broadcasted_iota(jnp.int32, n.shape, 1)
        first = (lane % (2 * half)) < half
        rot = jnp.where(first, pltpu.roll(n, QK_W - half, 1), pltpu.roll(n, half, 1))
        r = n * cos + rot * sin
        q_ref[...] = r[:, :N_Q_HEADS * HD].astype(BF16)
        k_ref[...] = _dup_halves(r[:, N_Q_HEADS * HD:QK_W]).astype(BF16)
        v_ref[...] = _with_ones(y[:, QK_W:ATT_W]).astype(BF16)

    softmax_mixer(OFF_B, gain_b_ref, cosb_ref, sinb_ref, HD // 2, qb_ref, kb_ref, vb_ref)
    softmax_mixer(OFF_C, gain_c_ref, cosc_ref, sinc_ref, HD // 4, qc_ref, kc_ref, vc_ref)


def _inproj(x2, g1, w_re, gain_b, gain_c, tabs, seq, tm):
    t = x2.shape[0]
    nt = seq // tm
    row = lambda i: (i, 0)
    fixed = lambda i: (0, 0)
    tab = lambda i: (i % nt, 0)
    kv_w = 2 * N_KV_HEADS * HD
    out_shape = (
        jax.ShapeDtypeStruct((t, PA_W), F32),
        jax.ShapeDtypeStruct((t, LANE), F32),
        jax.ShapeDtypeStruct((t, N_Q_HEADS * HD), BF16),
        jax.ShapeDtypeStruct((t, kv_w), BF16),
        jax.ShapeDtypeStruct((t, kv_w), BF16),
        jax.ShapeDtypeStruct((t, N_Q_HEADS * HD), BF16),
        jax.ShapeDtypeStruct((t, kv_w), BF16),
        jax.ShapeDtypeStruct((t, kv_w), BF16),
    )
    in_specs = [
        pl.BlockSpec((tm, D_MODEL), row),
        pl.BlockSpec((1, D_MODEL), fixed),
        pl.BlockSpec((D_MODEL, PROJ_W), fixed),
        pl.BlockSpec((1, QK_W), fixed),
        pl.BlockSpec((1, QK_W), fixed),
    ] + [pl.BlockSpec((tm, LANE), tab)] * 4
    out_specs = tuple(pl.BlockSpec((tm, s.shape[1]), row) for s in out_shape)
    return pl.pallas_call(
        _inproj_kernel, out_shape=out_shape, grid=(t // tm,),
        in_specs=in_specs, out_specs=out_specs,
        compiler_params=_params(("parallel",)), name="inproj",
    )(x2, g1, w_re, gain_b, gain_c, *tabs)


def _rope_tables(seq):
    def tables(pos, dim):
        inv = ROPE_THETA ** (-jnp.arange(0, dim, 2, dtype=F32) / dim)
        ang = pos.astype(F32)[:, None] * inv[None, :]
        cos = jnp.concatenate([jnp.cos(ang), jnp.cos(ang)], axis=-1)
        sin = jnp.concatenate([-jnp.sin(ang), jnp.sin(ang)], axis=-1)
        return cos, sin
    pos = jnp.arange(seq)
    cos1, sin1 = tables(pos, HD)
    cosr, sinr = tables(pos // GRID_W, HD // 2)
    cosc, sinc = tables(pos % GRID_W, HD // 2)
    cos2 = jnp.concatenate([cosr, cosc], axis=-1)
    sin2 = jnp.concatenate([sinr, sinc], axis=-1)
    two = lambda a: jnp.concatenate([a, a], axis=-1)
    return two(cos1), two(sin1), two(cos2), two(sin2)


NEG_BIG = -1e30


def _attn_kernel(sink_ref, q_ref, k_ref, v_ref, o_ref, *, tq, tk, seq, window):
    qi = pl.program_id(1)
    q0 = qi * tq
    if window:
        lo = jnp.maximum(q0 - WINDOW, 0) // tk
        hi = (jnp.minimum(q0 + tq + WINDOW, seq) + tk - 1) // tk
    else:
        lo, hi = 0, seq // tk
    lane = lax.broadcasted_iota(jnp.int32, (tq, LANE), 1)
    row2 = lax.broadcasted_iota(jnp.int32, (2 * tq, 1), 0)
    if window:
        qpos = q0 + row2 % tq
    groups = range(N_KV_HEADS)
    cols = [slice(g * LANE, (g + 1) * LANE) for g in groups]
    q2s, init = [], []
    for g in groups:
        qg = q_ref[:, cols[g]]
        zero = jnp.zeros_like(qg)
        q2s.append(jnp.concatenate([jnp.where(lane < HD, qg, zero),
                                    jnp.where(lane >= HD, qg, zero)], axis=0))
        lane2 = lax.broadcasted_iota(jnp.int32, (2 * tq, LANE), 1)
        if window:
            m0 = jnp.where(row2 < tq, sink_ref[2 * g], sink_ref[2 * g + 1])
            a0 = jnp.where(lane2 >= HD, 1.0, 0.0)
        else:
            m0 = jnp.full((2 * tq, 1), NEG_BIG, F32)
            a0 = jnp.zeros((2 * tq, LANE), F32)
        init.append((m0, a0))

    def body(c, carry):
        k0 = pl.multiple_of(c * tk, tk)
        out = []
        for g in groups:
            m, acc = carry[g]
            kc = k_ref[pl.ds(k0, tk), cols[g]]
            vc = v_ref[pl.ds(k0, tk), cols[g]]
            s = lax.dot_general(q2s[g], kc, (((1,), (1,)), ((), ())),
                                preferred_element_type=F32)
            if window:
                kpos = k0 + lax.broadcasted_iota(jnp.int32, (1, tk), 1)
                s = jnp.where(jnp.abs(qpos - kpos) <= WINDOW, s, NEG_BIG)
            m_new = jnp.maximum(m, jnp.max(s, axis=1, keepdims=True))
            alpha = jnp.exp2(m - m_new)
            p = jnp.exp2((s - m_new).astype(BF16))
            acc = alpha * acc + jnp.dot(p, vc, preferred_element_type=F32)
            out.append((m_new, acc))
        return tuple(out)

    res = lax.fori_loop(lo, hi, body, tuple(init))
    for g in groups:
        acc = res[g][1]
        o = acc / pltpu.roll(acc, HD, 1)
        o_ref[:, cols[g]] = jnp.where(lane < HD, o[:tq],
                                      pltpu.roll(o[tq:], HD, 1)).astype(o_ref.dtype)


def _attention(q, kd, vd, sinks, batch, seq, tq, tk, window):
    t = q.shape[0]
    nq = seq // tq
    width = N_Q_HEADS * HD
    kern = functools.partial(_attn_kernel, tq=tq, tk=tk, seq=seq, window=window)
    return pl.pallas_call(
        kern, out_shape=jax.ShapeDtypeStruct((t, width), BF16), grid=(batch, nq),
        in_specs=[pl.BlockSpec(memory_space=pltpu.SMEM),
                  pl.BlockSpec((tq, width), lambda b, i: (b * nq + i, 0)),
                  pl.BlockSpec((seq, width), lambda b, i: (b, 0)),
                  pl.BlockSpec((seq, width), lambda b, i: (b, 0))],
        out_specs=pl.BlockSpec((tq, width), lambda b, i: (b * nq + i, 0)),
        compiler_params=_params(("parallel", "parallel")),
        name="window_attention" if window else "dense_attention",
    )(sinks, q, kd, vd)


N_INST = 2 * A_HEADS
HALO = 8
HI = lax.Precision.HIGHEST


def _dot_hi(a, b):
    return jnp.dot(a, b, preferred_element_type=F32, precision=HI)


def _dot_bf(a, b):
    return jnp.dot(a.astype(BF16), b.astype(BF16), preferred_element_type=F32)


def _block_diag(m, reps, blk_r, blk_c):
    tall = jnp.concatenate([m] * reps, axis=0)
    r = lax.broadcasted_iota(jnp.int32, tall.shape, 0) // blk_r
    c = lax.broadcasted_iota(jnp.int32, tall.shape, 1) // blk_c
    return jnp.where(r == c, tall, jnp.zeros_like(tall))


def _unit_tri_inverse_stacked(low, reps):
    n = low.shape[0]
    r = lax.broadcasted_iota(jnp.int32, low.shape, 0)
    c = lax.broadcasted_iota(jnp.int32, low.shape, 1) % n
    p = -low
    x = jnp.where(r == c, 1.0, 0.0) + p
    for _ in range(int(math.log2(n)) - 1):
        p = jnp.dot(p.astype(BF16), _block_diag(p.astype(BF16), reps, n, n),
                    preferred_element_type=F32)
        x = x + jnp.dot(x.astype(BF16), _block_diag(p.astype(BF16), reps, n, n),
                        preferred_element_type=F32)
    return x


def _lane_blocks(pieces, n, width):
    total = len(pieces) * width
    lane = lax.broadcasted_iota(jnp.int32, (n, total), 1) // width
    out = jnp.broadcast_to(pieces[-1], (n, total))
    for j in range(len(pieces) - 2, -1, -1):
        out = jnp.where(lane == j, jnp.broadcast_to(pieces[j], (n, total)), out)
    return out


def _delta_prep_kernel(prev_ref, cur_ref, next_ref, gb_ref, cw_ref, aneg_ref, dtb_ref,
                       u_ref, w_ref, qg_ref, kd_ref, a_ref, dl_ref,
                       qs, ks, vs, gs, bs, *, tt, tiles_per_seq):
    i = pl.program_id(0) % tiles_per_seq
    prev = jnp.where(i == 0, 0.0, prev_ref[...])
    nxt = jnp.where(i == tiles_per_seq - 1, 0.0, next_ref[...])
    ext = jnp.concatenate([prev, cur_ref[...], nxt], axis=0)
    y = jnp.zeros((tt, QKV_A), F32)
    for j in range(A_CONV):
        s = HALO - A_CONV // 2 + j
        y = y + ext[s:s + tt, :] * cw_ref[j:j + 1, :]
    y = y / (1.0 + jnp.exp(-y))
    hw = A_HEADS * A_DK
    for h in range(A_HEADS):
        for src, dst, scale in ((0, qs, A_DK ** -0.5), (hw, ks, 1.0)):
            xh = y[:, src + h * A_DK: src + (h + 1) * A_DK]
            ss = jnp.sum(xh * xh, axis=1, keepdims=True)
            dst[:, h * A_DK:(h + 1) * A_DK] = xh * (lax.rsqrt(ss + EPS) * scale)
    vs[...] = y[:, 2 * hw:]
    gbv = gb_ref[...]
    bs[...] = 1.0 / (1.0 + jnp.exp(-gbv))
    xg = gbv + dtb_ref[...]
    softplus = jnp.maximum(xg, 0.0) + jnp.log1p(jnp.exp(-jnp.abs(xg)))
    gs[...] = aneg_ref[...] * softplus

    n = A_CHUNK
    nh = A_HEADS
    r = lax.broadcasted_iota(jnp.int32, (n, n), 0)
    c = lax.broadcasted_iota(jnp.int32, (n, n), 1)
    r4 = lax.broadcasted_iota(jnp.int32, (n, nh * n), 0)
    c4 = lax.broadcasted_iota(jnp.int32, (n, nh * n), 1) % n
    rt = lax.broadcasted_iota(jnp.int32, (nh * n, n), 0) % n
    ct = lax.broadcasted_iota(jnp.int32, (nh * n, n), 1)

    def chunk(ci, carry):
        r0 = pl.multiple_of(ci * n, n)
        rows = pl.ds(r0, n)
        g8 = gs[rows, :]
        b8 = bs[rows, :]
        q_all = qs[rows, :]
        k_all = ks[rows, :]
        v_all = vs[rows, :]
        gtot = jnp.sum(g8, axis=0, keepdims=True)
        dl_ref[pl.ds(ci, 1), :] = jnp.exp(gtot)
        k_bd = _block_diag(k_all.astype(BF16), nh, n, A_DK)
        for d in range(2):
            fwd = d == 0
            tri = jnp.where((c <= r) if fwd else (c >= r), 1.0, 0.0)
            tri_t4 = jnp.where((ct <= rt) if fwd else (ct >= rt), 1.0, 0.0)
            keep = (c4 <= r4) if fwd else (c4 >= r4)
            strict = (c4 < r4) if fwd else (c4 > r4)
            gc_all = _dot_hi(tri, g8)
            gc_row = lax.dot_general(g8, tri_t4, (((0,), (1,)), ((), ())),
                                     preferred_element_type=F32, precision=HI)
            lanes = [N_INST + d * nh + h for h in range(nh)]
            gcs = [gc_all[:, gl:gl + 1] for gl in lanes]
            egs = [jnp.exp(gc) for gc in gcs]
            betas = [b8[:, d * nh + h:d * nh + h + 1] for h in range(nh)]
            diff = (_lane_blocks(gcs, n, n)
                    - _lane_blocks([gc_row[gl:gl + 1, :] for gl in lanes], n, n))
            decay = jnp.where(keep, jnp.exp(jnp.where(keep, diff, 0.0)), 0.0)
            beta_w = _lane_blocks(betas, n, A_DK)
            eg_w = _lane_blocks(egs, n, A_DK)
            kb_all = k_all * beta_w
            both = lax.dot_general(jnp.concatenate([kb_all, q_all], axis=0).astype(BF16),
                                   k_bd, (((1,), (1,)), ((), ())),
                                   preferred_element_type=F32)
            tm = _unit_tri_inverse_stacked(jnp.where(strict, both[:n] * decay, 0.0), nh)
            vb_all = v_all * beta_w
            kbe_all = kb_all * eg_w
            rhs = jnp.concatenate(
                [jnp.concatenate([vb_all[:, h * A_DK:(h + 1) * A_DK],
                                  kbe_all[:, h * A_DK:(h + 1) * A_DK]], axis=1)
                 for h in range(nh)], axis=0)
            uw = jnp.dot(_block_diag(tm.astype(BF16), nh, n, n), rhs.astype(BF16),
                         preferred_element_type=F32)
            wide = slice(d * nh * A_DK, (d + 1) * nh * A_DK)
            for h in range(nh):
                o128 = slice((d * nh + h) * A_DK, (d * nh + h + 1) * A_DK)
                u_ref[rows, o128] = uw[h * n:(h + 1) * n, :A_DV]
                w_ref[rows, o128] = uw[h * n:(h + 1) * n, A_DV:]
            qg_ref[rows, wide] = q_all * eg_w
            kd_ref[rows, wide] = k_all * jnp.exp(
                _lane_blocks([gtot[:, gl:gl + 1] - gc for gl, gc in zip(lanes, gcs)], n, A_DK))
            a_ref[rows, d * nh * n:(d + 1) * nh * n] = both[n:] * decay
        return carry

    lax.fori_loop(0, tt // n, chunk, 0)


def _delta_prep(pa, gb, conv_w8, aneg, dtb, seq, tt):
    t = pa.shape[0]
    tps = seq // tt
    hb = tt // HALO
    last = t // HALO - 1
    wide = N_INST * A_DK
    kern = functools.partial(_delta_prep_kernel, tt=tt, tiles_per_seq=tps)
    big = jax.ShapeDtypeStruct((t, wide), F32)
    out_shape = (big, big, big, big,
                 jax.ShapeDtypeStruct((t, N_INST * A_CHUNK), F32),
                 jax.ShapeDtypeStruct((t // A_CHUNK, LANE), F32))
    row = lambda i: (i, 0)
    fixed = lambda i: (0, 0)
    return pl.pallas_call(
        kern, out_shape=out_shape, grid=(t // tt,),
        in_specs=[pl.BlockSpec((HALO, QKV_A), lambda i: (jnp.maximum(i * hb - 1, 0), 0)),
                  pl.BlockSpec((tt, QKV_A), row),
                  pl.BlockSpec((HALO, QKV_A), lambda i: (jnp.minimum((i + 1) * hb, last), 0)),
                  pl.BlockSpec((tt, LANE), row),
                  pl.BlockSpec((HALO, QKV_A), fixed),
                  pl.BlockSpec((1, LANE), fixed),
                  pl.BlockSpec((1, LANE), fixed)],
        out_specs=(pl.BlockSpec((tt, wide), row),) * 4
        + (pl.BlockSpec((tt, N_INST * A_CHUNK), row),
           pl.BlockSpec((tt // A_CHUNK, LANE), row)),
        scratch_shapes=[pltpu.VMEM((tt, A_HEADS * A_DK), F32)] * 3
        + [pltpu.VMEM((tt, LANE), F32)] * 2,
        compiler_params=_params(("parallel",)), name="delta_prep",
    )(pa, pa, pa, gb, conv_w8, aneg, dtb)


def _delta_scan_kernel(uf, wf, qf, kf, af, dlf, ub, wb, qb, kb, ab, dlb,
                       of_ref, ob_ref, state, *, cb):
    @pl.when(pl.program_id(1) == 0)
    def _():
        state[...] = jnp.zeros_like(state)

    n = A_CHUNK
    dirs = ((uf, wf, qf, kf, af, dlf, of_ref), (ub, wb, qb, kb, ab, dlb, ob_ref))

    def chunk(ci, carry):
        for d, (u, w, q, k, a, dl, o) in enumerate(dirs):
            cc = ci if d == 0 else cb - 1 - ci
            rows = pl.ds(pl.multiple_of(cc * n, n), n)
            dlrow = dl[pl.ds(cc, 1), :]
            for h in range(A_HEADS):
                inst = d * A_HEADS + h
                hs = slice(h * A_DK, (h + 1) * A_DK)
                st = state[inst]
                stb = st.astype(BF16)
                v_new = u[rows, hs] - jnp.dot(w[rows, hs].astype(BF16), stb,
                                              preferred_element_type=F32)
                vb16 = v_new.astype(BF16)
                o[rows, hs] = (jnp.dot(q[rows, hs].astype(BF16), stb, preferred_element_type=F32)
                               + jnp.dot(a[rows, h * n:(h + 1) * n].astype(BF16), vb16,
                                         preferred_element_type=F32))
                upd = lax.dot_general(k[rows, hs].astype(BF16), vb16, (((0,), (0,)), ((), ())),
                                      preferred_element_type=F32)
                gl = N_INST + inst
                state[inst] = st * dlrow[:, gl:gl + 1] + upd
        return carry

    lax.fori_loop(0, cb, chunk, 0)


def _delta_scan(u, w, qg, kd, a, dl, batch, seq, cb):
    t = u.shape[0]
    tt = cb * A_CHUNK
    nb = seq // tt
    hw = A_HEADS * A_DV
    fwd = lambda b, i: (b * nb + i, 0)
    bwd = lambda b, i: (b * nb + nb - 1 - i, 1)
    bwd0 = lambda b, i: (b * nb + nb - 1 - i, 0)
    big_f = pl.BlockSpec((tt, hw), fwd)
    big_b = pl.BlockSpec((tt, hw), bwd)
    aw = A_HEADS * A_CHUNK
    kern = functools.partial(_delta_scan_kernel, cb=cb)
    out = jax.ShapeDtypeStruct((t, hw), F32)
    return pl.pallas_call(
        kern, out_shape=(out, out), grid=(batch, nb),
        in_specs=[big_f] * 4 + [pl.BlockSpec((tt, aw), fwd), pl.BlockSpec((cb, LANE), fwd)]
        + [big_b] * 4 + [pl.BlockSpec((tt, aw), bwd), pl.BlockSpec((cb, LANE), bwd0)],
        out_specs=(pl.BlockSpec((tt, hw), fwd), pl.BlockSpec((tt, hw), bwd0)),
        scratch_shapes=[pltpu.VMEM((N_INST, A_DK, A_DV), F32)],
        compiler_params=_params(("parallel", "arbitrary")), name="delta_scan",
    )(u, w, qg, kd, a, dl, u, w, qg, kd, a, dl)


def _outproj_kernel(of_ref, ob_ref, z_ref, wb_ref, dc_ref, ga_ref, wo_ref, x_ref, g2_ref, wr_ref,
                    x1_ref, h2_ref, aff_ref):
    o = of_ref[...] + ob_ref[...]
    z = z_ref[...]
    parts = []
    for h in range(A_HEADS):
        hs = slice(h * A_DV, (h + 1) * A_DV)
        oh = o[:, hs]
        ms = jnp.mean(oh * oh, axis=1, keepdims=True)
        zh = z[:, hs]
        parts.append((oh * lax.rsqrt(ms + EPS) * ga_ref[...] * (zh / (1.0 + jnp.exp(-zh)))
                      ).astype(BF16))
    mixed = jnp.concatenate(parts + [wb_ref[...], dc_ref[...]], axis=1)
    x1 = x_ref[...] + jnp.dot(mixed, wo_ref[...], preferred_element_type=F32)
    x1_ref[...] = x1
    ms = jnp.mean(x1 * x1, axis=1, keepdims=True)
    h2 = x1 * lax.rsqrt(ms + EPS) * g2_ref[...]
    h2_ref[...] = h2.astype(BF16)
    logits = lax.dot_general(wr_ref[...], h2, (((1,), (1,)), ((), ())),
                             preferred_element_type=F32, precision=HI)
    m = jnp.max(logits, axis=0, keepdims=True)
    p = jnp.exp(logits - m)
    aff_ref[0] = p / jnp.sum(p, axis=0, keepdims=True)


def _outproj(o_f, o_b, pa, wnd, dns, ga, w_out, x2, g2, wr_t, batch, seq, tm):
    t = x2.shape[0]
    nt = seq // tm
    row = lambda i: (i, 0)
    fixed = lambda i: (0, 0)
    hw = A_HEADS * A_DV
    aw = N_Q_HEADS * HD
    return pl.pallas_call(
        _outproj_kernel,
        out_shape=(jax.ShapeDtypeStruct((t, D_MODEL), F32),
                   jax.ShapeDtypeStruct((t, D_MODEL), BF16),
                   jax.ShapeDtypeStruct((batch, N_EXPERTS, seq), F32)),
        grid=(t // tm,),
        in_specs=[pl.BlockSpec((tm, hw), row), pl.BlockSpec((tm, hw), row),
                  pl.BlockSpec((tm, hw), lambda i: (i, QKV_A // hw)),
                  pl.BlockSpec((tm, aw), row), pl.BlockSpec((tm, aw), row),
                  pl.BlockSpec((1, A_DV), fixed),
                  pl.BlockSpec((D_MODEL, D_MODEL), fixed),
                  pl.BlockSpec((tm, D_MODEL), row),
                  pl.BlockSpec((1, D_MODEL), fixed),
                  pl.BlockSpec((N_EXPERTS, D_MODEL), fixed)],
        out_specs=(pl.BlockSpec((tm, D_MODEL), row), pl.BlockSpec((tm, D_MODEL), row),
                   pl.BlockSpec((1, N_EXPERTS, tm), lambda i: (i // nt, 0, i % nt))),
        compiler_params=_params(("parallel",)), name="outproj_router",
    )(o_f, o_b, pa, wnd, dns, ga, w_out, x2, g2, wr_t)


def _excl_cumsum(mask01, tri, ones, blk_strict):
    mb = mask01.astype(BF16)
    incl = jnp.dot(mb, tri, preferred_element_type=F32)
    tot = jnp.dot(mb, ones, preferred_element_type=F32)
    rowoff = jnp.dot(blk_strict, tot.astype(BF16), preferred_element_type=F32)
    return rowoff + incl - mask01, rowoff


def _select_kernel(aff_ref, pos_ref, gate_ref, off_ref, *, nr, cap):
    aff = aff_ref[0]
    rows = aff.shape[0]
    aff3 = aff.reshape(N_EXPERTS, nr, LANE)

    def count(mask):
        c = jnp.sum(jnp.where(mask, 1.0, 0.0), axis=2, keepdims=True)
        return jnp.sum(c, axis=1, keepdims=True)

    def as_float(word):
        return lax.bitcast_convert_type(word, F32)

    def search(i, thr):
        cand = thr | lax.shift_left(jnp.int32(1), 30 - i)
        return jnp.where(count(aff3 >= as_float(cand)) >= cap, cand, thr)

    thr = lax.fori_loop(0, 31, search, jnp.zeros((N_EXPERTS, 1, 1), jnp.int32))
    gt = aff3 >= as_float(thr + 1)
    eq = (aff3 >= as_float(thr)) & jnp.logical_not(gt)
    need = cap - count(gt)

    r = lax.broadcasted_iota(jnp.int32, (LANE, LANE), 0)
    c = lax.broadcasted_iota(jnp.int32, (LANE, LANE), 1)
    tri = jnp.where(r <= c, 1.0, 0.0).astype(BF16)
    ones = jnp.ones((LANE, LANE), BF16)
    rr = lax.broadcasted_iota(jnp.int32, (rows, rows), 0)
    cc = lax.broadcasted_iota(jnp.int32, (rows, rows), 1)
    blk_strict = jnp.where((rr // nr == cc // nr) & (cc < rr), 1.0, 0.0).astype(BF16)

    eq01 = jnp.where(eq, 1.0, 0.0).reshape(rows, LANE)
    eq_rank, _ = _excl_cumsum(eq01, tri, ones, blk_strict)
    sel = gt | (eq & (eq_rank.reshape(N_EXPERTS, nr, LANE) < need))
    sel01 = jnp.where(sel, 1.0, 0.0).reshape(rows, LANE)
    pos, rowoff = _excl_cumsum(sel01, tri, ones, blk_strict)
    pos_ref[0] = jnp.where(sel01 > 0.0, pos, -1.0).astype(jnp.int32)
    gate_ref[0] = jnp.where(sel01 > 0.0, aff, 0.0)
    off_ref[0] = rowoff.astype(jnp.int32)


def _select(aff_t, batch, seq):
    nr = seq // LANE
    rows = N_EXPERTS * nr
    cap = CAP_FACTOR * seq // N_EXPERTS
    aff3 = aff_t.reshape(batch, rows, LANE)
    spec = pl.BlockSpec((1, rows, LANE), lambda b: (b, 0, 0))
    kern = functools.partial(_select_kernel, nr=nr, cap=cap)
    return pl.pallas_call(
        kern,
        out_shape=(jax.ShapeDtypeStruct((batch, rows, LANE), jnp.int32),
                   jax.ShapeDtypeStruct((batch, rows, LANE), F32),
                   jax.ShapeDtypeStruct((batch, rows, LANE), jnp.int32)),
        grid=(batch,), in_specs=[spec], out_specs=(spec, spec, spec),
        compiler_params=_params(("parallel",)), name="expert_select",
    )(aff3)


TOK_BLK = 2 * LANE
WIN = 64


def _tok_lanes(ref, b0, row):
    two = ref[b0, pl.ds(row, 2), :]
    return jnp.concatenate([two[0:1], two[1:2]], axis=1)


def _moe_up_kernel(boff_ref, h2_ref, pos_ref, gate_ref, wg_ref, wu_ref, hid_ref, xs, gr,
                   *, nblk, cap, ftile):
    b = pl.program_id(0)
    e = pl.program_id(1)
    base = (b * N_EXPERTS + e) * (nblk + 1)
    xs[...] = jnp.zeros_like(xs)
    gr[...] = jnp.zeros_like(gr)
    riota = lax.broadcasted_iota(jnp.int32, (WIN, TOK_BLK), 0)

    def block(j, carry):
        off = boff_ref[base + j]
        cnt = boff_ref[base + j + 1] - off
        r0 = (off // 8) * 8
        nwin = jnp.where(cnt > 0, (off + cnt - r0 + WIN - 1) // WIN, 0)
        prow = _tok_lanes(pos_ref, 0, 2 * j)
        grow = _tok_lanes(gate_ref, 0, 2 * j)
        toks = h2_ref[0, pl.ds(pl.multiple_of(j * TOK_BLK, TOK_BLK), TOK_BLK), :]

        def window(k, c2):
            rb = pl.multiple_of(r0 + k * WIN, 8)
            hit = (riota + rb) == prow
            onehot = jnp.where(hit, 1.0, 0.0).astype(BF16)
            xs[pl.ds(rb, WIN), :] += jnp.dot(onehot, toks, preferred_element_type=F32)
            g = jnp.sum(jnp.where(hit, grow, 0.0), axis=1, keepdims=True)
            gr[pl.ds(rb, WIN), :] += jnp.broadcast_to(g, (WIN, LANE))
            return c2

        lax.fori_loop(0, nwin, window, 0)
        return carry

    lax.fori_loop(0, nblk, block, 0)
    xsb = xs[0:cap, :].astype(BF16)
    gate = jnp.concatenate([gr[0:cap, :]] * (ftile // LANE), axis=1)
    for f in range(D_EXPERT // ftile):
        fs = slice(f * ftile, (f + 1) * ftile)
        g = jnp.dot(xsb, wg_ref[0, :, fs], preferred_element_type=F32)
        u = jnp.dot(xsb, wu_ref[0, :, fs], preferred_element_type=F32)
        hid_ref[0, 0, :, fs] = (g / (1.0 + jnp.exp(-g)) * u * gate).astype(BF16)


def _moe_up(boff, h2, pos, gate, wg, wu, batch, seq):
    nr = seq // LANE
    nblk = seq // TOK_BLK
    cap = CAP_FACTOR * seq // N_EXPERTS
    kern = functools.partial(_moe_up_kernel, nblk=nblk, cap=cap, ftile=512)
    grid_spec = pltpu.PrefetchScalarGridSpec(
        num_scalar_prefetch=1, grid=(batch, N_EXPERTS),
        in_specs=[pl.BlockSpec((1, seq, D_MODEL), lambda b, e, s: (b, 0, 0)),
                  pl.BlockSpec((1, nr, LANE), lambda b, e, s: (b, e, 0)),
                  pl.BlockSpec((1, nr, LANE), lambda b, e, s: (b, e, 0)),
                  pl.BlockSpec((1, D_MODEL, D_EXPERT), lambda b, e, s: (e, 0, 0)),
                  pl.BlockSpec((1, D_MODEL, D_EXPERT), lambda b, e, s: (e, 0, 0))],
        out_specs=pl.BlockSpec((1, 1, cap, D_EXPERT), lambda b, e, s: (b, e, 0, 0)),
        scratch_shapes=[pltpu.VMEM((cap + WIN, D_MODEL), F32),
                        pltpu.VMEM((cap + WIN, LANE), F32)])
    return pl.pallas_call(
        kern, out_shape=jax.ShapeDtypeStruct((batch, N_EXPERTS, cap, D_EXPERT), BF16),
        grid_spec=grid_spec,
        compiler_params=_params(("arbitrary", "arbitrary")), name="moe_gather_up",
    )(boff, h2, pos, gate, wg, wu)


GROUP = 4


def _moe_down_kernel(boff_ref, hid_ref, wd_ref, pos_ref, x1_ref, out_ref, ys,
                     *, nblk, nr, cap):
    b = pl.program_id(0)
    e = pl.program_id(2)

    @pl.when(e == 0)
    def _():
        ys[:, cap:, :] = jnp.zeros((N_EXPERTS, WIN, ys.shape[2]), ys.dtype)

    ys[e, 0:cap, :] = jnp.dot(hid_ref[0, 0], wd_ref[0], preferred_element_type=F32
                              ).astype(ys.dtype)

    @pl.when(e == N_EXPERTS - 1)
    def _():
        riota = lax.broadcasted_iota(jnp.int32, (WIN, TOK_BLK), 0)

        def block(j, carry):
            rows = pl.ds(pl.multiple_of(j * TOK_BLK, TOK_BLK), TOK_BLK)
            acc = x1_ref[0, rows, :]
            for g0 in range(0, N_EXPERTS, GROUP):
                starts, prows, npass = [], [], 0
                for e2 in range(g0, g0 + GROUP):
                    base = (b * N_EXPERTS + e2) * (nblk + 1)
                    off = boff_ref[base + j]
                    cnt = boff_ref[base + j + 1] - off
                    r0 = (off // 16) * 16
                    nwin = jnp.where(cnt > 0, (off + cnt - r0 + WIN - 1) // WIN, 0)
                    npass = jnp.maximum(npass, nwin)
                    starts.append(r0)
                    prows.append(_tok_lanes(pos_ref, 0, e2 * nr + 2 * j))

                def one_pass(k, a):
                    hots, vals = [], []
                    for i2, e2 in enumerate(range(g0, g0 + GROUP)):
                        rb = pl.multiple_of(jnp.minimum(starts[i2] + k * WIN, cap), 16)
                        hots.append(jnp.where((riota + rb) == prows[i2], 1.0, 0.0).astype(BF16))
                        vals.append(ys[e2, pl.ds(rb, WIN), :])
                    return a + lax.dot_general(
                        jnp.concatenate(hots, axis=0), jnp.concatenate(vals, axis=0),
                        (((0,), (0,)), ((), ())), preferred_element_type=F32)

                acc = lax.fori_loop(0, npass, one_pass, acc)
            out_ref[0, rows, :] = acc
            return carry

        lax.fori_loop(0, nblk, block, 0)


def _moe_down(boff, hid, wd, pos, x1, batch, seq, td):
    nr = seq // LANE
    nblk = seq // TOK_BLK
    cap = CAP_FACTOR * seq // N_EXPERTS
    kern = functools.partial(_moe_down_kernel, nblk=nblk, nr=nr, cap=cap)
    grid_spec = pltpu.PrefetchScalarGridSpec(
        num_scalar_prefetch=1, grid=(batch, D_MODEL // td, N_EXPERTS),
        in_specs=[pl.BlockSpec((1, 1, cap, D_EXPERT), lambda b, d, e, s: (b, e, 0, 0)),
                  pl.BlockSpec((1, D_EXPERT, td), lambda b, d, e, s: (e, 0, d)),
                  pl.BlockSpec((1, N_EXPERTS * nr, LANE), lambda b, d, e, s: (b, 0, 0)),
                  pl.BlockSpec((1, seq, td), lambda b, d, e, s: (b, 0, d))],
        out_specs=pl.BlockSpec((1, seq, td), lambda b, d, e, s: (b, 0, d)),
        scratch_shapes=[pltpu.VMEM((N_EXPERTS, cap + WIN, td), BF16)])
    return pl.pallas_call(
        kern, out_shape=jax.ShapeDtypeStruct((batch, seq, D_MODEL), F32),
        grid_spec=grid_spec,
        compiler_params=_params(("arbitrary", "arbitrary", "arbitrary")),
        name="moe_down_scatter",
    )(boff, hid, wd, pos, x1)


def _block_offsets(rowoff, batch, seq):
    nr = seq // LANE
    cap = CAP_FACTOR * seq // N_EXPERTS
    per_row = rowoff[:, :, 0].reshape(batch, N_EXPERTS, nr)
    step = TOK_BLK // LANE
    offs = jnp.concatenate([per_row[:, :, ::step],
                            jnp.full((batch, N_EXPERTS, 1), cap, jnp.int32)], axis=2)
    return offs.reshape(-1)


def _delta_params(conv_w, a_log, dt_bias):
    cw8 = jnp.concatenate([conv_w.astype(F32),
                           jnp.zeros((HALO - A_CONV, conv_w.shape[1]), F32)], axis=0)
    pad_l = jnp.zeros((N_INST,), F32)
    pad_r = jnp.zeros((LANE - 2 * N_INST,), F32)
    aneg = jnp.concatenate([pad_l, -jnp.exp(a_log.astype(F32)).reshape(-1), pad_r])[None, :]
    dtb = jnp.concatenate([pad_l, dt_bias.astype(F32).reshape(-1), pad_r])[None, :]
    return cw8, aneg, dtb


def _reorder_w_in(w):
    d = w.shape[0]
    gates = w[:, PA_W:PA_W + 4 * A_HEADS]
    rest = w[:, PA_W + 4 * A_HEADS:]
    pad = jnp.zeros((d, PROJ_W - OFF_G - 4 * A_HEADS), w.dtype)
    return jnp.concatenate([w[:, :PA_W], rest, gates, pad], axis=1)


def _qk_gains(gq, gk):
    scale = (HD ** -0.5) * math.log2(math.e)
    return jnp.concatenate([jnp.tile(gq.astype(F32) * scale, N_Q_HEADS),
                            jnp.tile(gk.astype(F32), N_KV_HEADS)])[None, :]


def _tiles(seq):
    tm = min(512, seq)
    return dict(tm=tm, tq=min(256, seq), tk_dense=min(512, seq), tk_win=LANE,
                tt=tm, cb=tm // A_CHUNK, td=256)


def _layer(x, p, l, tabs, batch, seq):
    ts = _tiles(seq)
    x2 = x.reshape(batch * seq, D_MODEL)
    w_re = _reorder_w_in(p["w_in"][l]).astype(BF16)
    gain_b = _qk_gains(p["b_qnorm_g"][l], p["b_knorm_g"][l])
    gain_c = _qk_gains(p["c_qnorm_g"][l], p["c_knorm_g"][l])
    pa, gb, qb, kb, vb, qc, kc, vc = _inproj(
        x2, p["norm1_g"][l].astype(F32)[None, :], w_re, gain_b, gain_c, tabs, seq, ts["tm"])

    cw8, aneg, dtb = _delta_params(p["conv_w"][l], p["a_log"][l], p["dt_bias"][l])
    u, w, qg, kd, a, dl = _delta_prep(pa, gb, cw8, aneg, dtb, seq, ts["tt"])
    o_f, o_b = _delta_scan(u, w, qg, kd, a, dl, batch, seq, ts["cb"])

    sinks = p["b_sink"][l].astype(F32) * math.log2(math.e)
    wnd = _attention(qb, kb, vb, sinks, batch, seq, ts["tq"], ts["tk_win"], True)
    dns = _attention(qc, kc, vc, sinks, batch, seq, ts["tq"], ts["tk_dense"], False)

    x1, h2, aff_t = _outproj(
        o_f, o_b, pa, wnd, dns, p["a_norm_g"][l].astype(F32)[None, :],
        p["w_out"][l].astype(BF16), x2, p["norm2_g"][l].astype(F32)[None, :],
        p["w_router"][l].astype(F32).T, batch, seq, ts["tm"])

    pos, gate, rowoff = _select(aff_t, batch, seq)
    boff = _block_offsets(rowoff, batch, seq)
    hid = _moe_up(boff, h2.reshape(batch, seq, D_MODEL), pos, gate,
                  p["w_gate"][l].astype(BF16), p["w_up"][l].astype(BF16), batch, seq)
    return _moe_down(boff, hid, p["w_down"][l].astype(BF16), pos,
                     x1.reshape(batch, seq, D_MODEL), batch, seq, ts["td"])


def kernel(x, norm1_g, w_in, conv_w, a_log, dt_bias, a_norm_g, b_qnorm_g, b_knorm_g, b_sink,
           c_qnorm_g, c_knorm_g, w_out, norm2_g, w_router, w_gate, w_up, w_down):
    batch, seq, _ = x.shape
    p = dict(norm1_g=norm1_g, w_in=w_in, conv_w=conv_w, a_log=a_log, dt_bias=dt_bias,
             a_norm_g=a_norm_g, b_qnorm_g=b_qnorm_g, b_knorm_g=b_knorm_g, b_sink=b_sink,
             c_qnorm_g=c_qnorm_g, c_knorm_g=c_knorm_g, w_out=w_out, norm2_g=norm2_g,
             w_router=w_router, w_gate=w_gate, w_up=w_up, w_down=w_down)
    tabs = _rope_tables(seq)
    for l in range(w_in.shape[0]):
        x = _layer(x, p, l, tabs, batch, seq)
    return x
```

```python
import functools
import math

import jax
import jax.numpy as jnp
from jax import lax
from jax.experimental import pallas as pl
from jax.experimental.pallas import tpu as pltpu

F32 = jnp.float32
BF16 = jnp.bfloat16

D_MODEL = 1024
GRID_W = 64
EPS = 1e-6
ROPE_THETA = 10000.0
A_HEADS = 4
A_DK = 128
A_DV = 128
A_CONV = 5
A_CHUNK = 64
HD = 64
N_Q_HEADS = 4
N_KV_HEADS = 2
WINDOW = 128
N_EXPERTS = 16
CAP_FACTOR = 2
D_EXPERT = 1024

QKV_A = 3 * A_HEADS * A_DK
PA_W = QKV_A + A_HEADS * A_DV
ATT_W = (N_Q_HEADS + 2 * N_KV_HEADS) * HD
QK_W = (N_Q_HEADS + N_KV_HEADS) * HD
OFF_B = PA_W
OFF_C = PA_W + ATT_W
OFF_G = PA_W + 2 * ATT_W
LANE = 128
PROJ_W = OFF_G + LANE

VMEM_LIMIT = 56 * 1024 * 1024


def _params(sem, vmem=None):
    return pltpu.CompilerParams(dimension_semantics=sem,
                                vmem_limit_bytes=vmem or VMEM_LIMIT)


def _group_sumsq(v, width):
    n = v.shape[-1]
    r = lax.broadcasted_iota(jnp.int32, (n, n), 0) // width
    c = lax.broadcasted_iota(jnp.int32, (n, n), 1) // width
    ones = jnp.where(r == c, 1.0, 0.0).astype(BF16)
    sq = v * v
    hi = sq.astype(BF16)
    lo = (sq - hi.astype(F32)).astype(BF16)
    return (jnp.dot(hi, ones, preferred_element_type=F32)
            + jnp.dot(lo, ones, preferred_element_type=F32))


def _dup_halves(v):
    lane = lax.broadcasted_iota(jnp.int32, v.shape, 1)
    sw = pltpu.roll(v, 64, 1)
    lo = lane < 64
    return jnp.concatenate([jnp.where(lo, v, sw), jnp.where(lo, sw, v)], axis=1)


def _with_ones(v):
    lane = lax.broadcasted_iota(jnp.int32, v.shape, 1)
    lo = lane < 64
    return jnp.concatenate([jnp.where(lo, v, 1.0),
                            jnp.where(lo, pltpu.roll(v, 64, 1), 1.0)], axis=1)


def _inproj_kernel(x_ref, g1_ref, w_ref, gain_b_ref, gain_c_ref,
                   cosb_ref, sinb_ref, cosc_ref, sinc_ref,
                   pa_ref, gb_ref, qb_ref, kb_ref, vb_ref, qc_ref, kc_ref, vc_ref):
    x = x_ref[...]
    ms = jnp.mean(x * x, axis=-1, keepdims=True)
    h = (x * lax.rsqrt(ms + EPS) * g1_ref[...]).astype(BF16)
    pa_ref[...] = jnp.dot(h, w_ref[:, 0:PA_W], preferred_element_type=F32)
    gb_ref[...] = jnp.dot(h, w_ref[:, OFF_G:PROJ_W], preferred_element_type=F32)

    def softmax_mixer(off, gain_ref, cos_ref, sin_ref, half, q_ref, k_ref, v_ref):
        y = jnp.dot(h, w_ref[:, off:off + ATT_W], preferred_element_type=F32)
        qk = y[:, :QK_W]
        ss = _group_sumsq(qk, HD)
        n = qk * lax.rsqrt(ss * (1.0 / HD) + EPS) * gain_ref[...]
        cos = jnp.concatenate([cos_ref[...]] * (QK_W // LANE), axis=1)
        sin = jnp.concatenate([sin_ref[...]] * (QK_W // LANE), axis=1)
        lane = lax.broadcasted_iota(jnp.int32, n.shape, 1)
        first = (lane % (2 * half)) < half
        rot = jnp.where(first, pltpu.roll(n, QK_W - half, 1), pltpu.roll(n, half, 1))
        r = n * cos + rot * sin
        q_ref[...] = r[:, :N_Q_HEADS * HD].astype(BF16)
        k_ref[...] = _dup_halves(r[:, N_Q_HEADS * HD:QK_W]).astype(BF16)
        v_ref[...] = _with_ones(y[:, QK_W:ATT_W]).astype(BF16)

    softmax_mixer(OFF_B, gain_b_ref, cosb_ref, sinb_ref, HD // 2, qb_ref, kb_ref, vb_ref)
    softmax_mixer(OFF_C, gain_c_ref, cosc_ref, sinc_ref, HD // 4, qc_ref, kc_ref, vc_ref)


def _inproj(x2, g1, w_re, gain_b, gain_c, tabs, seq, tm):
    t = x2.shape[0]
    nt = seq // tm
    row = lambda i: (i, 0)
    fixed = lambda i: (0, 0)
    tab = lambda i: (i % nt, 0)
    kv_w = 2 * N_KV_HEADS * HD
    out_shape = (
        jax.ShapeDtypeStruct((t, PA_W), F32),
        jax.ShapeDtypeStruct((t, LANE), F32),
        jax.ShapeDtypeStruct((t, N_Q_HEADS * HD), BF16),
        jax.ShapeDtypeStruct((t, kv_w), BF16),
        jax.ShapeDtypeStruct((t, kv_w), BF16),
        jax.ShapeDtypeStruct((t, N_Q_HEADS * HD), BF16),
        jax.ShapeDtypeStruct((t, kv_w), BF16),
        jax.ShapeDtypeStruct((t, kv_w), BF16),
    )
    in_specs = [
        pl.BlockSpec((tm, D_MODEL), row),
        pl.BlockSpec((1, D_MODEL), fixed),
        pl.BlockSpec((D_MODEL, PROJ_W), fixed),
        pl.BlockSpec((1, QK_W), fixed),
        pl.BlockSpec((1, QK_W), fixed),
    ] + [pl.BlockSpec((tm, LANE), tab)] * 4
    out_specs = tuple(pl.BlockSpec((tm, s.shape[1]), row) for s in out_shape)
    return pl.pallas_call(
        _inproj_kernel, out_shape=out_shape, grid=(t // tm,),
        in_specs=in_specs, out_specs=out_specs,
        compiler_params=_params(("parallel",)), name="inproj",
    )(x2, g1, w_re, gain_b, gain_c, *tabs)


def _rope_tables(seq):
    def tables(pos, dim):
        inv = ROPE_THETA ** (-jnp.arange(0, dim, 2, dtype=F32) / dim)
        ang = pos.astype(F32)[:, None] * inv[None, :]
        cos = jnp.concatenate([jnp.cos(ang), jnp.cos(ang)], axis=-1)
        sin = jnp.concatenate([-jnp.sin(ang), jnp.sin(ang)], axis=-1)
        return cos, sin
    pos = jnp.arange(seq)
    cos1, sin1 = tables(pos, HD)
    cosr, sinr = tables(pos // GRID_W, HD // 2)
    cosc, sinc = tables(pos % GRID_W, HD // 2)
    cos2 = jnp.concatenate([cosr, cosc], axis=-1)
    sin2 = jnp.concatenate([sinr, sinc], axis=-1)
    two = lambda a: jnp.concatenate([a, a], axis=-1)
    return two(cos1), two(sin1), two(cos2), two(sin2)


NEG_BIG = -1e30


def _attn_kernel(sink_ref, q_ref, k_ref, v_ref, o_ref, *, tq, tk, seq, window):
    qi = pl.program_id(1)
    q0 = qi * tq
    if window:
        lo = jnp.maximum(q0 - WINDOW, 0) // tk
        hi = (jnp.minimum(q0 + tq + WINDOW, seq) + tk - 1) // tk
    else:
        lo, hi = 0, seq // tk
    lane = lax.broadcasted_iota(jnp.int32, (tq, LANE), 1)
    row2 = lax.broadcasted_iota(jnp.int32, (2 * tq, 1), 0)
    if window:
        qpos = q0 + row2 % tq
    groups = range(N_KV_HEADS)
    cols = [slice(g * LANE, (g + 1) * LANE) for g in groups]
    q2s, init = [], []
    for g in groups:
        qg = q_ref[:, cols[g]]
        zero = jnp.zeros_like(qg)
        q2s.append(jnp.concatenate([jnp.where(lane < HD, qg, zero),
                                    jnp.where(lane >= HD, qg, zero)], axis=0))
        lane2 = lax.broadcasted_iota(jnp.int32, (2 * tq, LANE), 1)
        if window:
            m0 = jnp.where(row2 < tq, sink_ref[2 * g], sink_ref[2 * g + 1])
            a0 = jnp.where(lane2 >= HD, 1.0, 0.0)
        else:
            m0 = jnp.full((2 * tq, 1), NEG_BIG, F32)
            a0 = jnp.zeros((2 * tq, LANE), F32)
        init.append((m0, a0))

    def body(c, carry):
        k0 = pl.multiple_of(c * tk, tk)
        out = []
        for g in groups:
            m, acc = carry[g]
            kc = k_ref[pl.ds(k0, tk), cols[g]]
            vc = v_ref[pl.ds(k0, tk), cols[g]]
            s = lax.dot_general(q2s[g], kc, (((1,), (1,)), ((), ())),
                                preferred_element_type=F32)
            if window:
                kpos = k0 + lax.broadcasted_iota(jnp.int32, (1, tk), 1)
                s = jnp.where(jnp.abs(qpos - kpos) <= WINDOW, s, NEG_BIG)
            m_new = jnp.maximum(m, jnp.max(s, axis=1, keepdims=True))
            alpha = jnp.exp2(m - m_new)
            p = jnp.exp2((s - m_new).astype(BF16))
            acc = alpha * acc + jnp.dot(p, vc, preferred_element_type=F32)
            out.append((m_new, acc))
        return tuple(out)

    res = lax.fori_loop(lo, hi, body, tuple(init))
    for g in groups:
        acc = res[g][1]
        o = acc / pltpu.roll(acc, HD, 1)
        o_ref[:, cols[g]] = jnp.where(lane < HD, o[:tq],
                                      pltpu.roll(o[tq:], HD, 1)).astype(o_ref.dtype)


def _attention(q, kd, vd, sinks, batch, seq, tq, tk, window):
    t = q.shape[0]
    nq = seq // tq
    width = N_Q_HEADS * HD
    kern = functools.partial(_attn_kernel, tq=tq, tk=tk, seq=seq, window=window)
    return pl.pallas_call(
        kern, out_shape=jax.ShapeDtypeStruct((t, width), BF16), grid=(batch, nq),
        in_specs=[pl.BlockSpec(memory_space=pltpu.SMEM),
                  pl.BlockSpec((tq, width), lambda b, i: (b * nq + i, 0)),
                  pl.BlockSpec((seq, width), lambda b, i: (b, 0)),
                  pl.BlockSpec((seq, width), lambda b, i: (b, 0))],
        out_specs=pl.BlockSpec((tq, width), lambda b, i: (b * nq + i, 0)),
        compiler_params=_params(("parallel", "parallel")),
        name="window_attention" if window else "dense_attention",
    )(sinks, q, kd, vd)


N_INST = 2 * A_HEADS
HALO = 8
HI = lax.Precision.HIGHEST


def _dot_hi(a, b):
    return jnp.dot(a, b, preferred_element_type=F32, precision=HI)


def _dot_bf(a, b):
    return jnp.dot(a.astype(BF16), b.astype(BF16), preferred_element_type=F32)


def _block_diag(m, reps, blk_r, blk_c):
    tall = jnp.concatenate([m] * reps, axis=0)
    r = lax.broadcasted_iota(jnp.int32, tall.shape, 0) // blk_r
    c = lax.broadcasted_iota(jnp.int32, tall.shape, 1) // blk_c
    return jnp.where(r == c, tall, jnp.zeros_like(tall))


def _unit_tri_inverse_stacked(low, reps):
    n = low.shape[0]
    r = lax.broadcasted_iota(jnp.int32, low.shape, 0)
    c = lax.broadcasted_iota(jnp.int32, low.shape, 1) % n
    p = -low
    x = jnp.where(r == c, 1.0, 0.0) + p
    for _ in range(int(math.log2(n)) - 1):
        p = jnp.dot(p.astype(BF16), _block_diag(p.astype(BF16), reps, n, n),
                    preferred_element_type=F32)
        x = x + jnp.dot(x.astype(BF16), _block_diag(p.astype(BF16), reps, n, n),
                        preferred_element_type=F32)
    return x


def _lane_blocks(pieces, n, width):
    total = len(pieces) * width
    lane = lax.broadcasted_iota(jnp.int32, (n, total), 1) // width
    out = jnp.broadcast_to(pieces[-1], (n, total))
    for j in range(len(pieces) - 2, -1, -1):
        out = jnp.where(lane == j, jnp.broadcast_to(pieces[j], (n, total)), out)
    return out


def _delta_prep_kernel(prev_ref, cur_ref, next_ref, gb_ref, cw_ref, aneg_ref, dtb_ref,
                       u_ref, w_ref, qg_ref, kd_ref, a_ref, dl_ref,
                       qs, ks, vs, gs, bs, *, tt, tiles_per_seq):
    i = pl.program_id(0) % tiles_per_seq
    prev = jnp.where(i == 0, 0.0, prev_ref[...])
    nxt = jnp.where(i == tiles_per_seq - 1, 0.0, next_ref[...])
    ext = jnp.concatenate([prev, cur_ref[...], nxt], axis=0)
    y = jnp.zeros((tt, QKV_A), F32)
    for j in range(A_CONV):
        s = HALO - A_CONV // 2 + j
        y = y + ext[s:s + tt, :] * cw_ref[j:j + 1, :]
    y = y / (1.0 + jnp.exp(-y))
    hw = A_HEADS * A_DK
    for h in range(A_HEADS):
        for src, dst, scale in ((0, qs, A_DK ** -0.5), (hw, ks, 1.0)):
            xh = y[:, src + h * A_DK: src + (h + 1) * A_DK]
            ss = jnp.sum(xh * xh, axis=1, keepdims=True)
            dst[:, h * A_DK:(h + 1) * A_DK] = xh * (lax.rsqrt(ss + EPS) * scale)
    vs[...] = y[:, 2 * hw:]
    gbv = gb_ref[...]
    bs[...] = 1.0 / (1.0 + jnp.exp(-gbv))
    xg = gbv + dtb_ref[...]
    softplus = jnp.maximum(xg, 0.0) + jnp.log1p(jnp.exp(-jnp.abs(xg)))
    gs[...] = aneg_ref[...] * softplus

    n = A_CHUNK
    nh = A_HEADS
    r = lax.broadcasted_iota(jnp.int32, (n, n), 0)
    c = lax.broadcasted_iota(jnp.int32, (n, n), 1)
    r4 = lax.broadcasted_iota(jnp.int32, (n, nh * n), 0)
    c4 = lax.broadcasted_iota(jnp.int32, (n, nh * n), 1) % n
    rt = lax.broadcasted_iota(jnp.int32, (nh * n, n), 0) % n
    ct = lax.broadcasted_iota(jnp.int32, (nh * n, n), 1)

    def chunk(ci, carry):
        r0 = pl.multiple_of(ci * n, n)
        rows = pl.ds(r0, n)
        g8 = gs[rows, :]
        b8 = bs[rows, :]
        q_all = qs[rows, :]
        k_all = ks[rows, :]
        v_all = vs[rows, :]
        gtot = jnp.sum(g8, axis=0, keepdims=True)
        dl_ref[pl.ds(ci, 1), :] = jnp.exp(gtot)
        k_bd = _block_diag(k_all.astype(BF16), nh, n, A_DK)
        for d in range(2):
            fwd = d == 0
            tri = jnp.where((c <= r) if fwd else (c >= r), 1.0, 0.0)
            tri_t4 = jnp.where((ct <= rt) if fwd else (ct >= rt), 1.0, 0.0)
            keep = (c4 <= r4) if fwd else (c4 >= r4)
            strict = (c4 < r4) if fwd else (c4 > r4)
            gc_all = _dot_hi(tri, g8)
            gc_row = lax.dot_general(g8, tri_t4, (((0,), (1,)), ((), ())),
                                     preferred_element_type=F32, precision=HI)
            lanes = [N_INST + d * nh + h for h in range(nh)]
            gcs = [gc_all[:, gl:gl + 1] for gl in lanes]
            egs = [jnp.exp(gc) for gc in gcs]
            betas = [b8[:, d * nh + h:d * nh + h + 1] for h in range(nh)]
            diff = (_lane_blocks(gcs, n, n)
                    - _lane_blocks([gc_row[gl:gl + 1, :] for gl in lanes], n, n))
            decay = jnp.where(keep, jnp.exp(jnp.where(keep, diff, 0.0)), 0.0)
            beta_w = _lane_blocks(betas, n, A_DK)
            eg_w = _lane_blocks(egs, n, A_DK)
            kb_all = k_all * beta_w
            both = lax.dot_general(jnp.concatenate([kb_all, q_all], axis=0).astype(BF16),
                                   k_bd, (((1,), (1,)), ((), ())),
                                   preferred_element_type=F32)
            tm = _unit_tri_inverse_stacked(jnp.where(strict, both[:n] * decay, 0.0), nh)
            vb_all = v_all * beta_w
            kbe_all = kb_all * eg_w
            rhs = jnp.concatenate(
                [jnp.concatenate([vb_all[:, h * A_DK:(h + 1) * A_DK],
                                  kbe_all[:, h * A_DK:(h + 1) * A_DK]], axis=1)
                 for h in range(nh)], axis=0)
            uw = jnp.dot(_block_diag(tm.astype(BF16), nh, n, n), rhs.astype(BF16),
                         preferred_element_type=F32)
            wide = slice(d * nh * A_DK, (d + 1) * nh * A_DK)
            for h in range(nh):
                o128 = slice((d * nh + h) * A_DK, (d * nh + h + 1) * A_DK)
                u_ref[rows, o128] = uw[h * n:(h + 1) * n, :A_DV]
                w_ref[rows, o128] = uw[h * n:(h + 1) * n, A_DV:]
            qg_ref[rows, wide] = q_all * eg_w
            kd_ref[rows, wide] = k_all * jnp.exp(
                _lane_blocks([gtot[:, gl:gl + 1] - gc for gl, gc in zip(lanes, gcs)], n, A_DK))
            a_ref[rows, d * nh * n:(d + 1) * nh * n] = both[n:] * decay
        return carry

    lax.fori_loop(0, tt // n, chunk, 0)


def _delta_prep(pa, gb, conv_w8, aneg, dtb, seq, tt):
    t = pa.shape[0]
    tps = seq // tt
    hb = tt // HALO
    last = t // HALO - 1
    wide = N_INST * A_DK
    kern = functools.partial(_delta_prep_kernel, tt=tt, tiles_per_seq=tps)
    big = jax.ShapeDtypeStruct((t, wide), F32)
    out_shape = (big, big, big, big,
                 jax.ShapeDtypeStruct((t, N_INST * A_CHUNK), F32),
                 jax.ShapeDtypeStruct((t // A_CHUNK, LANE), F32))
    row = lambda i: (i, 0)
    fixed = lambda i: (0, 0)
    return pl.pallas_call(
        kern, out_shape=out_shape, grid=(t // tt,),
        in_specs=[pl.BlockSpec((HALO, QKV_A), lambda i: (jnp.maximum(i * hb - 1, 0), 0)),
                  pl.BlockSpec((tt, QKV_A), row),
                  pl.BlockSpec((HALO, QKV_A), lambda i: (jnp.minimum((i + 1) * hb, last), 0)),
                  pl.BlockSpec((tt, LANE), row),
                  pl.BlockSpec((HALO, QKV_A), fixed),
                  pl.BlockSpec((1, LANE), fixed),
                  pl.BlockSpec((1, LANE), fixed)],
        out_specs=(pl.BlockSpec((tt, wide), row),) * 4
        + (pl.BlockSpec((tt, N_INST * A_CHUNK), row),
           pl.BlockSpec((tt // A_CHUNK, LANE), row)),
        scratch_shapes=[pltpu.VMEM((tt, A_HEADS * A_DK), F32)] * 3
        + [pltpu.VMEM((tt, LANE), F32)] * 2,
        compiler_params=_params(("parallel",)), name="delta_prep",
    )(pa, pa, pa, gb, conv_w8, aneg, dtb)


def _delta_scan_kernel(uf, wf, qf, kf, af, dlf, ub, wb, qb, kb, ab, dlb,
                       of_ref, ob_ref, state, *, cb):
    @pl.when(pl.program_id(1) == 0)
    def _():
        state[...] = jnp.zeros_like(state)

    n = A_CHUNK
    dirs = ((uf, wf, qf, kf, af, dlf, of_ref), (ub, wb, qb, kb, ab, dlb, ob_ref))

    def chunk(ci, carry):
        for d, (u, w, q, k, a, dl, o) in enumerate(dirs):
            cc = ci if d == 0 else cb - 1 - ci
            rows = pl.ds(pl.multiple_of(cc * n, n), n)
            dlrow = dl[pl.ds(cc, 1), :]
            for h in range(A_HEADS):
                inst = d * A_HEADS + h
                hs = slice(h * A_DK, (h + 1) * A_DK)
                st = state[inst]
                stb = st.astype(BF16)
                v_new = u[rows, hs] - jnp.dot(w[rows, hs].astype(BF16), stb,
                                              preferred_element_type=F32)
                vb16 = v_new.astype(BF16)
                o[rows, hs] = (jnp.dot(q[rows, hs].astype(BF16), stb, preferred_element_type=F32)
                               + jnp.dot(a[rows, h * n:(h + 1) * n].astype(BF16), vb16,
                                         preferred_element_type=F32))
                upd = lax.dot_general(k[rows, hs].astype(BF16), vb16, (((0,), (0,)), ((), ())),
                                      preferred_element_type=F32)
                gl = N_INST + inst
                state[inst] = st * dlrow[:, gl:gl + 1] + upd
        return carry

    lax.fori_loop(0, cb, chunk, 0)


def _delta_scan(u, w, qg, kd, a, dl, batch, seq, cb):
    t = u.shape[0]
    tt = cb * A_CHUNK
    nb = seq // tt
    hw = A_HEADS * A_DV
    fwd = lambda b, i: (b * nb + i, 0)
    bwd = lambda b, i: (b * nb + nb - 1 - i, 1)
    bwd0 = lambda b, i: (b * nb + nb - 1 - i, 0)
    big_f = pl.BlockSpec((tt, hw), fwd)
    big_b = pl.BlockSpec((tt, hw), bwd)
    aw = A_HEADS * A_CHUNK
    kern = functools.partial(_delta_scan_kernel, cb=cb)
    out = jax.ShapeDtypeStruct((t, hw), F32)
    return pl.pallas_call(
        kern, out_shape=(out, out), grid=(batch, nb),
        in_specs=[big_f] * 4 + [pl.BlockSpec((tt, aw), fwd), pl.BlockSpec((cb, LANE), fwd)]
        + [big_b] * 4 + [pl.BlockSpec((tt, aw), bwd), pl.BlockSpec((cb, LANE), bwd0)],
        out_specs=(pl.BlockSpec((tt, hw), fwd), pl.BlockSpec((tt, hw), bwd0)),
        scratch_shapes=[pltpu.VMEM((N_INST, A_DK, A_DV), F32)],
        compiler_params=_params(("parallel", "arbitrary")), name="delta_scan",
    )(u, w, qg, kd, a, dl, u, w, qg, kd, a, dl)


def _outproj_kernel(of_ref, ob_ref, z_ref, wb_ref, dc_ref, ga_ref, wo_ref, x_ref, g2_ref, wr_ref,
                    x1_ref, h2_ref, aff_ref):
    o = of_ref[...] + ob_ref[...]
    z = z_ref[...]
    parts = []
    for h in range(A_HEADS):
        hs = slice(h * A_DV, (h + 1) * A_DV)
        oh = o[:, hs]
        ms = jnp.mean(oh * oh, axis=1, keepdims=True)
        zh = z[:, hs]
        parts.append((oh * lax.rsqrt(ms + EPS) * ga_ref[...] * (zh / (1.0 + jnp.exp(-zh)))
                      ).astype(BF16))
    mixed = jnp.concatenate(parts + [wb_ref[...], dc_ref[...]], axis=1)
    x1 = x_ref[...] + jnp.dot(mixed, wo_ref[...], preferred_element_type=F32)
    x1_ref[...] = x1
    ms = jnp.mean(x1 * x1, axis=1, keepdims=True)
    h2 = x1 * lax.rsqrt(ms + EPS) * g2_ref[...]
    h2_ref[...] = h2.astype(BF16)
    logits = lax.dot_general(wr_ref[...], h2, (((1,), (1,)), ((), ())),
                             preferred_element_type=F32, precision=HI)
    m = jnp.max(logits, axis=0, keepdims=True)
    p = jnp.exp(logits - m)
    aff_ref[0] = p / jnp.sum(p, axis=0, keepdims=True)


def _outproj(o_f, o_b, pa, wnd, dns, ga, w_out, x2, g2, wr_t, batch, seq, tm):
    t = x2.shape[0]
    nt = seq // tm
    row = lambda i: (i, 0)
    fixed = lambda i: (0, 0)
    hw = A_HEADS * A_DV
    aw = N_Q_HEADS * HD
    return pl.pallas_call(
        _outproj_kernel,
        out_shape=(jax.ShapeDtypeStruct((t, D_MODEL), F32),
                   jax.ShapeDtypeStruct((t, D_MODEL), BF16),
                   jax.ShapeDtypeStruct((batch, N_EXPERTS, seq), F32)),
        grid=(t // tm,),
        in_specs=[pl.BlockSpec((tm, hw), row), pl.BlockSpec((tm, hw), row),
                  pl.BlockSpec((tm, hw), lambda i: (i, QKV_A // hw)),
                  pl.BlockSpec((tm, aw), row), pl.BlockSpec((tm, aw), row),
                  pl.BlockSpec((1, A_DV), fixed),
                  pl.BlockSpec((D_MODEL, D_MODEL), fixed),
                  pl.BlockSpec((tm, D_MODEL), row),
                  pl.BlockSpec((1, D_MODEL), fixed),
                  pl.BlockSpec((N_EXPERTS, D_MODEL), fixed)],
        out_specs=(pl.BlockSpec((tm, D_MODEL), row), pl.BlockSpec((tm, D_MODEL), row),
                   pl.BlockSpec((1, N_EXPERTS, tm), lambda i: (i // nt, 0, i % nt))),
        compiler_params=_params(("parallel",)), name="outproj_router",
    )(o_f, o_b, pa, wnd, dns, ga, w_out, x2, g2, wr_t)


def _excl_cumsum(mask01, tri, ones, blk_strict):
    mb = mask01.astype(BF16)
    incl = jnp.dot(mb, tri, preferred_element_type=F32)
    tot = jnp.dot(mb, ones, preferred_element_type=F32)
    rowoff = jnp.dot(blk_strict, tot.astype(BF16), preferred_element_type=F32)
    return rowoff + incl - mask01, rowoff


def _select_kernel(aff_ref, pos_ref, gate_ref, off_ref, *, nr, cap):
    aff = aff_ref[0]
    rows = aff.shape[0]
    aff3 = aff.reshape(N_EXPERTS, nr, LANE)

    def count(mask):
        c = jnp.sum(jnp.where(mask, 1.0, 0.0), axis=2, keepdims=True)
        return jnp.sum(c, axis=1, keepdims=True)

    def as_float(word):
        return lax.bitcast_convert_type(word, F32)

    def search(i, thr):
        cand = thr | lax.shift_left(jnp.int32(1), 30 - i)
        return jnp.where(count(aff3 >= as_float(cand)) >= cap, cand, thr)

    thr = lax.fori_loop(0, 31, search, jnp.zeros((N_EXPERTS, 1, 1), jnp.int32))
    gt = aff3 >= as_float(thr + 1)
    eq = (aff3 >= as_float(thr)) & jnp.logical_not(gt)
    need = cap - count(gt)

    r = lax.broadcasted_iota(jnp.int32, (LANE, LANE), 0)
    c = lax.broadcasted_iota(jnp.int32, (LANE, LANE), 1)
    tri = jnp.where(r <= c, 1.0, 0.0).astype(BF16)
    ones = jnp.ones((LANE, LANE), BF16)
    rr = lax.broadcasted_iota(jnp.int32, (rows, rows), 0)
    cc = lax.broadcasted_iota(jnp.int32, (rows, rows), 1)
    blk_strict = jnp.where((rr // nr == cc // nr) & (cc < rr), 1.0, 0.0).astype(BF16)

    eq01 = jnp.where(eq, 1.0, 0.0).reshape(rows, LANE)
    eq_rank, _ = _excl_cumsum(eq01, tri, ones, blk_strict)
    sel = gt | (eq & (eq_rank.reshape(N_EXPERTS, nr, LANE) < need))
    sel01 = jnp.where(sel, 1.0, 0.0).reshape(rows, LANE)
    pos, rowoff = _excl_cumsum(sel01, tri, ones, blk_strict)
    pos_ref[0] = jnp.where(sel01 > 0.0, pos, -1.0).astype(jnp.int32)
    gate_ref[0] = jnp.where(sel01 > 0.0, aff, 0.0)
    off_ref[0] = rowoff.astype(jnp.int32)


def _select(aff_t, batch, seq):
    nr = seq // LANE
    rows = N_EXPERTS * nr
    cap = CAP_FACTOR * seq // N_EXPERTS
    aff3 = aff_t.reshape(batch, rows, LANE)
    spec = pl.BlockSpec((1, rows, LANE), lambda b: (b, 0, 0))
    kern = functools.partial(_select_kernel, nr=nr, cap=cap)
    return pl.pallas_call(
        kern,
        out_shape=(jax.ShapeDtypeStruct((batch, rows, LANE), jnp.int32),
                   jax.ShapeDtypeStruct((batch, rows, LANE), F32),
                   jax.ShapeDtypeStruct((batch, rows, LANE), jnp.int32)),
        grid=(batch,), in_specs=[spec], out_specs=(spec, spec, spec),
        compiler_params=_params(("parallel",)), name="expert_select",
    )(aff3)


TOK_BLK = 2 * LANE
WIN = 64


def _tok_lanes(ref, b0, row):
    two = ref[b0, pl.ds(row, 2), :]
    return jnp.concatenate([two[0:1], two[1:2]], axis=1)


def _moe_up_kernel(boff_ref, h2_ref, pos_ref, gate_ref, wg_ref, wu_ref, hid_ref, xs, gr,
                   *, nblk, cap, ftile):
    b = pl.program_id(0)
    e = pl.program_id(1)
    base = (b * N_EXPERTS + e) * (nblk + 1)
    xs[...] = jnp.zeros_like(xs)
    gr[...] = jnp.zeros_like(gr)
    riota = lax.broadcasted_iota(jnp.int32, (WIN, TOK_BLK), 0)

    def block(j, carry):
        off = boff_ref[base + j]
        cnt = boff_ref[base + j + 1] - off
        r0 = (off // 8) * 8
        nwin = jnp.where(cnt > 0, (off + cnt - r0 + WIN - 1) // WIN, 0)
        prow = _tok_lanes(pos_ref, 0, 2 * j)
        grow = _tok_lanes(gate_ref, 0, 2 * j)
        toks = h2_ref[0, pl.ds(pl.multiple_of(j * TOK_BLK, TOK_BLK), TOK_BLK), :]

        def window(k, c2):
            rb = pl.multiple_of(r0 + k * WIN, 8)
            hit = (riota + rb) == prow
            onehot = jnp.where(hit, 1.0, 0.0).astype(BF16)
            xs[pl.ds(rb, WIN), :] += jnp.dot(onehot, toks, preferred_element_type=F32)
            g = jnp.sum(jnp.where(hit, grow, 0.0), axis=1, keepdims=True)
            gr[pl.ds(rb, WIN), :] += jnp.broadcast_to(g, (WIN, LANE))
            return c2

        lax.fori_loop(0, nwin, window, 0)
        return carry

    lax.fori_loop(0, nblk, block, 0)
    xsb = xs[0:cap, :].astype(BF16)
    gate = jnp.concatenate([gr[0:cap, :]] * (ftile // LANE), axis=1)
    for f in range(D_EXPERT // ftile):
        fs = slice(f * ftile, (f + 1) * ftile)
        g = jnp.dot(xsb, wg_ref[0, 0, :, fs].astype(BF16), preferred_element_type=F32)
        u = jnp.dot(xsb, wu_ref[0, 0, :, fs].astype(BF16), preferred_element_type=F32)
        hid_ref[0, 0, :, fs] = (g / (1.0 + jnp.exp(-g)) * u * gate).astype(BF16)


def _moe_up(boff, h2, pos, gate, wg, wu, layer, batch, seq):
    nr = seq // LANE
    nblk = seq // TOK_BLK
    cap = CAP_FACTOR * seq // N_EXPERTS
    kern = functools.partial(_moe_up_kernel, nblk=nblk, cap=cap, ftile=512)
    grid_spec = pltpu.PrefetchScalarGridSpec(
        num_scalar_prefetch=1, grid=(batch, N_EXPERTS),
        in_specs=[pl.BlockSpec((1, seq, D_MODEL), lambda b, e, s: (b, 0, 0),
                               pipeline_mode=pl.Buffered(1)),
                  pl.BlockSpec((1, nr, LANE), lambda b, e, s: (b, e, 0)),
                  pl.BlockSpec((1, nr, LANE), lambda b, e, s: (b, e, 0)),
                  pl.BlockSpec((1, 1, D_MODEL, D_EXPERT), lambda b, e, s: (layer, e, 0, 0)),
                  pl.BlockSpec((1, 1, D_MODEL, D_EXPERT), lambda b, e, s: (layer, e, 0, 0))],
        out_specs=pl.BlockSpec((1, 1, cap, D_EXPERT), lambda b, e, s: (b, e, 0, 0)),
        scratch_shapes=[pltpu.VMEM((cap + WIN, D_MODEL), F32),
                        pltpu.VMEM((cap + WIN, LANE), F32)])
    return pl.pallas_call(
        kern, out_shape=jax.ShapeDtypeStruct((batch, N_EXPERTS, cap, D_EXPERT), BF16),
        grid_spec=grid_spec,
        compiler_params=_params(("arbitrary", "arbitrary")), name="moe_gather_up",
    )(boff, h2, pos, gate, wg, wu)


GROUP = 4


def _moe_down_kernel(boff_ref, hid_ref, wd_ref, pos_ref, x1_ref, out_ref, ys,
                     *, nblk, nr, cap):
    b = pl.program_id(0)
    e = pl.program_id(2)

    @pl.when(e == 0)
    def _():
        ys[:, cap:, :] = jnp.zeros((N_EXPERTS, WIN, ys.shape[2]), ys.dtype)

    ys[e, 0:cap, :] = jnp.dot(hid_ref[0, 0], wd_ref[0, 0].astype(BF16), preferred_element_type=F32
                              ).astype(ys.dtype)

    @pl.when(e == N_EXPERTS - 1)
    def _():
        riota = lax.broadcasted_iota(jnp.int32, (WIN, TOK_BLK), 0)

        def block(j, carry):
            rows = pl.ds(pl.multiple_of(j * TOK_BLK, TOK_BLK), TOK_BLK)
            acc = x1_ref[0, rows, :]
            for g0 in range(0, N_EXPERTS, GROUP):
                starts, prows, npass = [], [], 0
                for e2 in range(g0, g0 + GROUP):
                    base = (b * N_EXPERTS + e2) * (nblk + 1)
                    off = boff_ref[base + j]
                    cnt = boff_ref[base + j + 1] - off
                    r0 = (off // 16) * 16
                    nwin = jnp.where(cnt > 0, (off + cnt - r0 + WIN - 1) // WIN, 0)
                    npass = jnp.maximum(npass, nwin)
                    starts.append(r0)
                    prows.append(_tok_lanes(pos_ref, 0, e2 * nr + 2 * j))

                def one_pass(k, a):
                    hots, vals = [], []
                    for i2, e2 in enumerate(range(g0, g0 + GROUP)):
                        rb = pl.multiple_of(jnp.minimum(starts[i2] + k * WIN, cap), 16)
                        hots.append(jnp.where((riota + rb) == prows[i2], 1.0, 0.0).astype(BF16))
                        vals.append(ys[e2, pl.ds(rb, WIN), :])
                    return a + lax.dot_general(
                        jnp.concatenate(hots, axis=0), jnp.concatenate(vals, axis=0),
                        (((0,), (0,)), ((), ())), preferred_element_type=F32)

                acc = lax.fori_loop(0, npass, one_pass, acc)
            out_ref[0, rows, :] = acc
            return carry

        lax.fori_loop(0, nblk, block, 0)


def _moe_down(boff, hid, wd, pos, x1, layer, batch, seq, td):
    nr = seq // LANE
    nblk = seq // TOK_BLK
    cap = CAP_FACTOR * seq // N_EXPERTS
    kern = functools.partial(_moe_down_kernel, nblk=nblk, nr=nr, cap=cap)
    grid_spec = pltpu.PrefetchScalarGridSpec(
        num_scalar_prefetch=1, grid=(batch, D_MODEL // td, N_EXPERTS),
        in_specs=[pl.BlockSpec((1, 1, cap, D_EXPERT), lambda b, d, e, s: (b, e, 0, 0)),
                  pl.BlockSpec((1, 1, D_EXPERT, td), lambda b, d, e, s: (layer, e, 0, d)),
                  pl.BlockSpec((1, N_EXPERTS * nr, LANE), lambda b, d, e, s: (b, 0, 0)),
                  pl.BlockSpec((1, seq, td), lambda b, d, e, s: (b, 0, d))],
        out_specs=pl.BlockSpec((1, seq, td), lambda b, d, e, s: (b, 0, d)),
        scratch_shapes=[pltpu.VMEM((N_EXPERTS, cap + WIN, td), BF16)])
    return pl.pallas_call(
        kern, out_shape=jax.ShapeDtypeStruct((batch, seq, D_MODEL), F32),
        grid_spec=grid_spec,
        compiler_params=_params(("arbitrary", "arbitrary", "arbitrary")),
        name="moe_down_scatter",
    )(boff, hid, wd, pos, x1)


def _block_offsets(rowoff, batch, seq):
    nr = seq // LANE
    cap = CAP_FACTOR * seq // N_EXPERTS
    per_row = rowoff[:, :, 0].reshape(batch, N_EXPERTS, nr)
    step = TOK_BLK // LANE
    offs = jnp.concatenate([per_row[:, :, ::step],
                            jnp.full((batch, N_EXPERTS, 1), cap, jnp.int32)], axis=2)
    return offs.reshape(-1)


def _delta_params(conv_w, a_log, dt_bias):
    cw8 = jnp.concatenate([conv_w.astype(F32),
                           jnp.zeros((HALO - A_CONV, conv_w.shape[1]), F32)], axis=0)
    pad_l = jnp.zeros((N_INST,), F32)
    pad_r = jnp.zeros((LANE - 2 * N_INST,), F32)
    aneg = jnp.concatenate([pad_l, -jnp.exp(a_log.astype(F32)).reshape(-1), pad_r])[None, :]
    dtb = jnp.concatenate([pad_l, dt_bias.astype(F32).reshape(-1), pad_r])[None, :]
    return cw8, aneg, dtb


def _reorder_w_in(w):
    d = w.shape[0]
    gates = w[:, PA_W:PA_W + 4 * A_HEADS]
    rest = w[:, PA_W + 4 * A_HEADS:]
    pad = jnp.zeros((d, PROJ_W - OFF_G - 4 * A_HEADS), w.dtype)
    return jnp.concatenate([w[:, :PA_W], rest, gates, pad], axis=1)


def _qk_gains(gq, gk):
    scale = (HD ** -0.5) * math.log2(math.e)
    return jnp.concatenate([jnp.tile(gq.astype(F32) * scale, N_Q_HEADS),
                            jnp.tile(gk.astype(F32), N_KV_HEADS)])[None, :]


def _tiles(seq):
    tm = min(512, seq)
    return dict(tm=tm, tq=min(256, seq), tq_dense=min(256, seq), tk_dense=min(2048, seq),
                tk_win=LANE,
                tt=tm, cb=tm // A_CHUNK, td=256)


def _layer(x, p, l, tabs, batch, seq):
    ts = _tiles(seq)
    x2 = x.reshape(batch * seq, D_MODEL)
    w_re = _reorder_w_in(p["w_in"][l]).astype(BF16)
    gain_b = _qk_gains(p["b_qnorm_g"][l], p["b_knorm_g"][l])
    gain_c = _qk_gains(p["c_qnorm_g"][l], p["c_knorm_g"][l])
    pa, gb, qb, kb, vb, qc, kc, vc = _inproj(
        x2, p["norm1_g"][l].astype(F32)[None, :], w_re, gain_b, gain_c, tabs, seq, ts["tm"])

    cw8, aneg, dtb = _delta_params(p["conv_w"][l], p["a_log"][l], p["dt_bias"][l])
    u, w, qg, kd, a, dl = _delta_prep(pa, gb, cw8, aneg, dtb, seq, ts["tt"])
    o_f, o_b = _delta_scan(u, w, qg, kd, a, dl, batch, seq, ts["cb"])

    sinks = p["b_sink"][l].astype(F32) * math.log2(math.e)
    wnd = _attention(qb, kb, vb, sinks, batch, seq, ts["tq"], ts["tk_win"], True)
    dns = _attention(qc, kc, vc, sinks, batch, seq, ts["tq_dense"], ts["tk_dense"], False)

    x1, h2, aff_t = _outproj(
        o_f, o_b, pa, wnd, dns, p["a_norm_g"][l].astype(F32)[None, :],
        p["w_out"][l].astype(BF16), x2, p["norm2_g"][l].astype(F32)[None, :],
        p["w_router"][l].astype(F32).T, batch, seq, ts["tm"])

    pos, gate, rowoff = _select(aff_t, batch, seq)
    boff = _block_offsets(rowoff, batch, seq)
    hid = _moe_up(boff, h2.reshape(batch, seq, D_MODEL), pos, gate,
                  p["w_gate"], p["w_up"], l, batch, seq)
    return _moe_down(boff, hid, p["w_down"], pos,
                     x1.reshape(batch, seq, D_MODEL), l, batch, seq, ts["td"])


def kernel(x, norm1_g, w_in, conv_w, a_log, dt_bias, a_norm_g, b_qnorm_g, b_knorm_g, b_sink,
           c_qnorm_g, c_knorm_g, w_out, norm2_g, w_router, w_gate, w_up, w_down):
    batch, seq, _ = x.shape
    p = dict(norm1_g=norm1_g, w_in=w_in, conv_w=conv_w, a_log=a_log, dt_bias=dt_bias,
             a_norm_g=a_norm_g, b_qnorm_g=b_qnorm_g, b_knorm_g=b_knorm_g, b_sink=b_sink,
             c_qnorm_g=c_qnorm_g, c_knorm_g=c_knorm_g, w_out=w_out, norm2_g=norm2_g,
             w_router=w_router, w_gate=w_gate, w_up=w_up, w_down=w_down)
    tabs = _rope_tables(seq)
    for l in range(w_in.shape[0]):
        x = _layer(x, p, l, tabs, batch, seq)
    return x
```

```python
import functools
import math

import jax
import jax.numpy as jnp
from jax import lax
from jax.experimental import pallas as pl
from jax.experimental.pallas import tpu as pltpu

F32 = jnp.float32
BF16 = jnp.bfloat16

D_MODEL = 1024
GRID_W = 64
EPS = 1e-6
ROPE_THETA = 10000.0
A_HEADS = 4
A_DK = 128
A_DV = 128
A_CONV = 5
A_CHUNK = 64
HD = 64
N_Q_HEADS = 4
N_KV_HEADS = 2
WINDOW = 128
N_EXPERTS = 16
CAP_FACTOR = 2
D_EXPERT = 1024

QKV_A = 3 * A_HEADS * A_DK
PA_W = QKV_A + A_HEADS * A_DV
ATT_W = (N_Q_HEADS + 2 * N_KV_HEADS) * HD
QK_W = (N_Q_HEADS + N_KV_HEADS) * HD
OFF_B = PA_W
OFF_C = PA_W + ATT_W
OFF_G = PA_W + 2 * ATT_W
LANE = 128
PROJ_W = OFF_G + LANE

VMEM_LIMIT = 56 * 1024 * 1024


def _params(sem, vmem=None):
    return pltpu.CompilerParams(dimension_semantics=sem,
                                vmem_limit_bytes=vmem or VMEM_LIMIT)


def _group_sumsq(v, width):
    n = v.shape[-1]
    r = lax.broadcasted_iota(jnp.int32, (n, n), 0) // width
    c = lax.broadcasted_iota(jnp.int32, (n, n), 1) // width
    ones = jnp.where(r == c, 1.0, 0.0).astype(BF16)
    sq = v * v
    hi = sq.astype(BF16)
    lo = (sq - hi.astype(F32)).astype(BF16)
    return (jnp.dot(hi, ones, preferred_element_type=F32)
            + jnp.dot(lo, ones, preferred_element_type=F32))


def _dup_halves(v):
    lane = lax.broadcasted_iota(jnp.int32, v.shape, 1)
    sw = pltpu.roll(v, 64, 1)
    lo = lane < 64
    return jnp.concatenate([jnp.where(lo, v, sw), jnp.where(lo, sw, v)], axis=1)


def _with_ones(v):
    lane = lax.broadcasted_iota(jnp.int32, v.shape, 1)
    lo = lane < 64
    return jnp.concatenate([jnp.where(lo, v, 1.0),
                            jnp.where(lo, pltpu.roll(v, 64, 1), 1.0)], axis=1)


def _inproj_kernel(x_ref, g1_ref, w_ref, gain_b_ref, gain_c_ref,
                   cosb_ref, sinb_ref, cosc_ref, sinc_ref,
                   pa_ref, gb_ref, qb_ref, kb_ref, vb_ref, qc_ref, kc_ref, vc_ref):
    x = x_ref[...]
    ms = jnp.mean(x * x, axis=-1, keepdims=True)
    h = (x * lax.rsqrt(ms + EPS) * g1_ref[...]).astype(BF16)
    pa_ref[...] = jnp.dot(h, w_ref[:, 0:PA_W], preferred_element_type=F32)
    gb_ref[...] = jnp.dot(h, w_ref[:, OFF_G:PROJ_W], preferred_element_type=F32)

    def softmax_mixer(off, gain_ref, cos_ref, sin_ref, half, q_ref, k_ref, v_ref):
        y = jnp.dot(h, w_ref[:, off:off + ATT_W], preferred_element_type=F32)
        qk = y[:, :QK_W]
        ss = _group_sumsq(qk, HD)
        n = qk * lax.rsqrt(ss * (1.0 / HD) + EPS) * gain_ref[...]
        cos = jnp.concatenate([cos_ref[...]] * (QK_W // LANE), axis=1)
        sin = jnp.concatenate([sin_ref[...]] * (QK_W // LANE), axis=1)
        lane = lax.broadcasted_iota(jnp.int32, n.shape, 1)
        first = (lane % (2 * half)) < half
        rot = jnp.where(first, pltpu.roll(n, QK_W - half, 1), pltpu.roll(n, half, 1))
        r = n * cos + rot * sin
        q_ref[...] = r[:, :N_Q_HEADS * HD].astype(BF16)
        k_ref[...] = _dup_halves(r[:, N_Q_HEADS * HD:QK_W]).astype(BF16)
        v_ref[...] = _with_ones(y[:, QK_W:ATT_W]).astype(BF16)

    softmax_mixer(OFF_B, gain_b_ref, cosb_ref, sinb_ref, HD // 2, qb_ref, kb_ref, vb_ref)
    softmax_mixer(OFF_C, gain_c_ref, cosc_ref, sinc_ref, HD // 4, qc_ref, kc_ref, vc_ref)


def _inproj(x2, g1, w_re, gain_b, gain_c, tabs, seq, tm):
    t = x2.shape[0]
    nt = seq // tm
    row = lambda i: (i, 0)
    fixed = lambda i: (0, 0)
    tab = lambda i: (i % nt, 0)
    kv_w = 2 * N_KV_HEADS * HD
    out_shape = (
        jax.ShapeDtypeStruct((t, PA_W), F32),
        jax.ShapeDtypeStruct((t, LANE), F32),
        jax.ShapeDtypeStruct((t, N_Q_HEADS * HD), BF16),
        jax.ShapeDtypeStruct((t, kv_w), BF16),
        jax.ShapeDtypeStruct((t, kv_w), BF16),
        jax.ShapeDtypeStruct((t, N_Q_HEADS * HD), BF16),
        jax.ShapeDtypeStruct((t, kv_w), BF16),
        jax.ShapeDtypeStruct((t, kv_w), BF16),
    )
    in_specs = [
        pl.BlockSpec((tm, D_MODEL), row),
        pl.BlockSpec((1, D_MODEL), fixed),
        pl.BlockSpec((D_MODEL, PROJ_W), fixed),
        pl.BlockSpec((1, QK_W), fixed),
        pl.BlockSpec((1, QK_W), fixed),
    ] + [pl.BlockSpec((tm, LANE), tab)] * 4
    out_specs = tuple(pl.BlockSpec((tm, s.shape[1]), row) for s in out_shape)
    return pl.pallas_call(
        _inproj_kernel, out_shape=out_shape, grid=(t // tm,),
        in_specs=in_specs, out_specs=out_specs,
        compiler_params=_params(("parallel",)), name="inproj",
    )(x2, g1, w_re, gain_b, gain_c, *tabs)


def _rope_tables(seq):
    def tables(pos, dim):
        inv = ROPE_THETA ** (-jnp.arange(0, dim, 2, dtype=F32) / dim)
        ang = pos.astype(F32)[:, None] * inv[None, :]
        cos = jnp.concatenate([jnp.cos(ang), jnp.cos(ang)], axis=-1)
        sin = jnp.concatenate([-jnp.sin(ang), jnp.sin(ang)], axis=-1)
        return cos, sin
    pos = jnp.arange(seq)
    cos1, sin1 = tables(pos, HD)
    cosr, sinr = tables(pos // GRID_W, HD // 2)
    cosc, sinc = tables(pos % GRID_W, HD // 2)
    cos2 = jnp.concatenate([cosr, cosc], axis=-1)
    sin2 = jnp.concatenate([sinr, sinc], axis=-1)
    two = lambda a: jnp.concatenate([a, a], axis=-1)
    return two(cos1), two(sin1), two(cos2), two(sin2)


NEG_BIG = -1e30


def _attn_kernel(sink_ref, q_ref, k_ref, v_ref, o_ref, *, tq, tk, seq, window):
    qi = pl.program_id(1)
    q0 = qi * tq
    lane = lax.broadcasted_iota(jnp.int32, (tq, LANE), 1)
    row2 = lax.broadcasted_iota(jnp.int32, (2 * tq, 1), 0)
    if window:
        qpos = q0 + row2 % tq
    groups = range(N_KV_HEADS)
    cols = [slice(g * LANE, (g + 1) * LANE) for g in groups]
    q2s, init = [], []
    for g in groups:
        qg = q_ref[:, cols[g]]
        zero = jnp.zeros_like(qg)
        q2s.append(jnp.concatenate([jnp.where(lane < HD, qg, zero),
                                    jnp.where(lane >= HD, qg, zero)], axis=0))
        lane2 = lax.broadcasted_iota(jnp.int32, (2 * tq, LANE), 1)
        if window:
            m0 = jnp.where(row2 < tq, sink_ref[2 * g], sink_ref[2 * g + 1])
            a0 = jnp.where(lane2 >= HD, 1.0, 0.0)
        else:
            m0 = jnp.full((2 * tq, 1), NEG_BIG, F32)
            a0 = jnp.zeros((2 * tq, LANE), F32)
        init.append((m0, a0))

    def step(k0, carry):
        out = []
        for g in groups:
            m, acc = carry[g]
            kc = k_ref[pl.ds(k0, tk), cols[g]]
            vc = v_ref[pl.ds(k0, tk), cols[g]]
            s = lax.dot_general(q2s[g], kc, (((1,), (1,)), ((), ())),
                                preferred_element_type=F32)
            if window:
                kpos = k0 + lax.broadcasted_iota(jnp.int32, (1, tk), 1)
                s = jnp.where(jnp.abs(qpos - kpos) <= WINDOW, s, NEG_BIG)
            m_new = jnp.maximum(m, jnp.max(s, axis=1, keepdims=True))
            alpha = jnp.exp2(m - m_new)
            p = jnp.exp2((s - m_new).astype(BF16))
            acc = alpha * acc + jnp.dot(p, vc, preferred_element_type=F32)
            out.append((m_new, acc))
        return tuple(out)

    if window:
        res = step(pl.multiple_of(jnp.clip(q0 - WINDOW, 0, seq - tk), LANE), tuple(init))
    else:
        res = lax.fori_loop(0, seq // tk,
                            lambda c, carry: step(pl.multiple_of(c * tk, tk), carry),
                            tuple(init))
    for g in groups:
        acc = res[g][1]
        o = acc / pltpu.roll(acc, HD, 1)
        o_ref[:, cols[g]] = jnp.where(lane < HD, o[:tq],
                                      pltpu.roll(o[tq:], HD, 1)).astype(o_ref.dtype)


def _attention(q, kd, vd, sinks, batch, seq, tq, tk, window):
    t = q.shape[0]
    nq = seq // tq
    width = N_Q_HEADS * HD
    kern = functools.partial(_attn_kernel, tq=tq, tk=tk, seq=seq, window=window)
    return pl.pallas_call(
        kern, out_shape=jax.ShapeDtypeStruct((t, width), BF16), grid=(batch, nq),
        in_specs=[pl.BlockSpec(memory_space=pltpu.SMEM),
                  pl.BlockSpec((tq, width), lambda b, i: (b * nq + i, 0)),
                  pl.BlockSpec((seq, width), lambda b, i: (b, 0)),
                  pl.BlockSpec((seq, width), lambda b, i: (b, 0))],
        out_specs=pl.BlockSpec((tq, width), lambda b, i: (b * nq + i, 0)),
        compiler_params=_params(("parallel", "parallel")),
        name="window_attention" if window else "dense_attention",
    )(sinks, q, kd, vd)


N_INST = 2 * A_HEADS
HALO = 8
HI = lax.Precision.HIGHEST


def _dot_hi(a, b):
    return jnp.dot(a, b, preferred_element_type=F32, precision=HI)


def _dot_bf(a, b):
    return jnp.dot(a.astype(BF16), b.astype(BF16), preferred_element_type=F32)


def _block_diag(m, reps, blk_r, blk_c):
    tall = jnp.concatenate([m] * reps, axis=0)
    r = lax.broadcasted_iota(jnp.int32, tall.shape, 0) // blk_r
    c = lax.broadcasted_iota(jnp.int32, tall.shape, 1) // blk_c
    return jnp.where(r == c, tall, jnp.zeros_like(tall))


def _unit_tri_inverse_stacked(low, reps):
    n = low.shape[0]
    r = lax.broadcasted_iota(jnp.int32, low.shape, 0)
    c = lax.broadcasted_iota(jnp.int32, low.shape, 1) % n
    p = -low
    x = jnp.where(r == c, 1.0, 0.0) + p
    for _ in range(int(math.log2(n)) - 1):
        p = jnp.dot(p.astype(BF16), _block_diag(p.astype(BF16), reps, n, n),
                    preferred_element_type=F32)
        x = x + jnp.dot(x.astype(BF16), _block_diag(p.astype(BF16), reps, n, n),
                        preferred_element_type=F32)
    return x


def _lane_blocks(pieces, n, width):
    total = len(pieces) * width
    lane = lax.broadcasted_iota(jnp.int32, (n, total), 1) // width
    out = jnp.broadcast_to(pieces[-1], (n, total))
    for j in range(len(pieces) - 2, -1, -1):
        out = jnp.where(lane == j, jnp.broadcast_to(pieces[j], (n, total)), out)
    return out


def _delta_prep_kernel(prev_ref, cur_ref, next_ref, gb_ref, cw_ref, aneg_ref, dtb_ref,
                       u_ref, w_ref, qg_ref, kd_ref, a_ref, dl_ref,
                       qs, ks, vs, gs, bs, ext, *, tt, tiles_per_seq):
    i = pl.program_id(0) % tiles_per_seq
    ext[0:HALO, :] = jnp.where(i == 0, 0.0, prev_ref[...])
    ext[HALO:HALO + tt, :] = cur_ref[...]
    ext[HALO + tt:, :] = jnp.where(i == tiles_per_seq - 1, 0.0, next_ref[...])
    y = jnp.zeros((tt, QKV_A), F32)
    for j in range(A_CONV):
        s = HALO - A_CONV // 2 + j
        y = y + ext[s:s + tt, :] * cw_ref[j:j + 1, :]
    y = y / (1.0 + jnp.exp(-y))
    hw = A_HEADS * A_DK
    for h in range(A_HEADS):
        for src, dst, scale in ((0, qs, A_DK ** -0.5), (hw, ks, 1.0)):
            xh = y[:, src + h * A_DK: src + (h + 1) * A_DK]
            ss = jnp.sum(xh * xh, axis=1, keepdims=True)
            dst[:, h * A_DK:(h + 1) * A_DK] = xh * (lax.rsqrt(ss + EPS) * scale)
    vs[...] = y[:, 2 * hw:]
    gbv = gb_ref[...]
    bs[...] = 1.0 / (1.0 + jnp.exp(-gbv))
    xg = gbv + dtb_ref[...]
    softplus = jnp.maximum(xg, 0.0) + jnp.log1p(jnp.exp(-jnp.abs(xg)))
    gs[...] = aneg_ref[...] * softplus

    n = A_CHUNK
    nh = A_HEADS
    r = lax.broadcasted_iota(jnp.int32, (n, n), 0)
    c = lax.broadcasted_iota(jnp.int32, (n, n), 1)
    r4 = lax.broadcasted_iota(jnp.int32, (n, nh * n), 0)
    c4 = lax.broadcasted_iota(jnp.int32, (n, nh * n), 1) % n
    rt = lax.broadcasted_iota(jnp.int32, (nh * n, n), 0) % n
    ct = lax.broadcasted_iota(jnp.int32, (nh * n, n), 1)

    def chunk(ci, carry):
        r0 = pl.multiple_of(ci * n, n)
        rows = pl.ds(r0, n)
        g8 = gs[rows, :]
        b8 = bs[rows, :]
        q_all = qs[rows, :]
        k_all = ks[rows, :]
        v_all = vs[rows, :]
        gtot = jnp.sum(g8, axis=0, keepdims=True)
        dl_ref[pl.ds(ci, 1), :] = jnp.exp(gtot)
        k_bd = _block_diag(k_all.astype(BF16), nh, n, A_DK)
        for d in range(2):
            fwd = d == 0
            tri = jnp.where((c <= r) if fwd else (c >= r), 1.0, 0.0)
            tri_t4 = jnp.where((ct <= rt) if fwd else (ct >= rt), 1.0, 0.0)
            keep = (c4 <= r4) if fwd else (c4 >= r4)
            strict = (c4 < r4) if fwd else (c4 > r4)
            gc_all = _dot_hi(tri, g8)
            gc_row = lax.dot_general(g8, tri_t4, (((0,), (1,)), ((), ())),
                                     preferred_element_type=F32, precision=HI)
            lanes = [N_INST + d * nh + h for h in range(nh)]
            gcs = [gc_all[:, gl:gl + 1] for gl in lanes]
            egs = [jnp.exp(gc) for gc in gcs]
            betas = [b8[:, d * nh + h:d * nh + h + 1] for h in range(nh)]
            diff = (_lane_blocks(gcs, n, n)
                    - _lane_blocks([gc_row[gl:gl + 1, :] for gl in lanes], n, n))
            decay = jnp.where(keep, jnp.exp(jnp.where(keep, diff, 0.0)), 0.0)
            beta_w = _lane_blocks(betas, n, A_DK)
            eg_w = _lane_blocks(egs, n, A_DK)
            kb_all = k_all * beta_w
            both = lax.dot_general(jnp.concatenate([kb_all, q_all], axis=0).astype(BF16),
                                   k_bd, (((1,), (1,)), ((), ())),
                                   preferred_element_type=F32)
            tm = _unit_tri_inverse_stacked(jnp.where(strict, both[:n] * decay, 0.0), nh)
            vb_all = v_all * beta_w
            kbe_all = kb_all * eg_w
            rhs = jnp.concatenate(
                [jnp.concatenate([vb_all[:, h * A_DK:(h + 1) * A_DK],
                                  kbe_all[:, h * A_DK:(h + 1) * A_DK]], axis=1)
                 for h in range(nh)], axis=0)
            uw = jnp.dot(_block_diag(tm.astype(BF16), nh, n, n), rhs.astype(BF16),
                         preferred_element_type=F32)
            wide = slice(d * nh * A_DK, (d + 1) * nh * A_DK)
            for h in range(nh):
                o128 = slice((d * nh + h) * A_DK, (d * nh + h + 1) * A_DK)
                u_ref[rows, o128] = uw[h * n:(h + 1) * n, :A_DV]
                w_ref[rows, o128] = uw[h * n:(h + 1) * n, A_DV:]
            qg_ref[rows, wide] = q_all * eg_w
            kd_ref[rows, wide] = k_all * jnp.exp(
                _lane_blocks([gtot[:, gl:gl + 1] - gc for gl, gc in zip(lanes, gcs)], n, A_DK))
            a_ref[rows, d * nh * n:(d + 1) * nh * n] = both[n:] * decay
        return carry

    lax.fori_loop(0, tt // n, chunk, 0)


def _delta_prep(pa, gb, conv_w8, aneg, dtb, seq, tt):
    t = pa.shape[0]
    tps = seq // tt
    hb = tt // HALO
    last = t // HALO - 1
    wide = N_INST * A_DK
    kern = functools.partial(_delta_prep_kernel, tt=tt, tiles_per_seq=tps)
    big = jax.ShapeDtypeStruct((t, wide), F32)
    out_shape = (big, big, big, big,
                 jax.ShapeDtypeStruct((t, N_INST * A_CHUNK), F32),
                 jax.ShapeDtypeStruct((t // A_CHUNK, LANE), F32))
    row = lambda i: (i, 0)
    fixed = lambda i: (0, 0)
    return pl.pallas_call(
        kern, out_shape=out_shape, grid=(t // tt,),
        in_specs=[pl.BlockSpec((HALO, QKV_A), lambda i: (jnp.maximum(i * hb - 1, 0), 0)),
                  pl.BlockSpec((tt, QKV_A), row),
                  pl.BlockSpec((HALO, QKV_A), lambda i: (jnp.minimum((i + 1) * hb, last), 0)),
                  pl.BlockSpec((tt, LANE), row),
                  pl.BlockSpec((HALO, QKV_A), fixed),
                  pl.BlockSpec((1, LANE), fixed),
                  pl.BlockSpec((1, LANE), fixed)],
        out_specs=(pl.BlockSpec((tt, wide), row),) * 4
        + (pl.BlockSpec((tt, N_INST * A_CHUNK), row),
           pl.BlockSpec((tt // A_CHUNK, LANE), row)),
        scratch_shapes=[pltpu.VMEM((tt, A_HEADS * A_DK), F32)] * 3
        + [pltpu.VMEM((tt, LANE), F32)] * 2 + [pltpu.VMEM((tt + 2 * HALO, QKV_A), F32)],
        compiler_params=_params(("parallel",)), name="delta_prep",
    )(pa, pa, pa, gb, conv_w8, aneg, dtb)


def _delta_scan_kernel(uf, wf, qf, kf, af, dlf, ub, wb, qb, kb, ab, dlb,
                       of_ref, ob_ref, state, *, cb):
    @pl.when(pl.program_id(1) == 0)
    def _():
        state[...] = jnp.zeros_like(state)

    n = A_CHUNK
    dirs = ((uf, wf, qf, kf, af, dlf, of_ref), (ub, wb, qb, kb, ab, dlb, ob_ref))

    def chunk(ci, carry):
        for d, (u, w, q, k, a, dl, o) in enumerate(dirs):
            cc = ci if d == 0 else cb - 1 - ci
            rows = pl.ds(pl.multiple_of(cc * n, n), n)
            dlrow = dl[pl.ds(cc, 1), :]
            for h in range(A_HEADS):
                inst = d * A_HEADS + h
                hs = slice(h * A_DK, (h + 1) * A_DK)
                st = state[inst]
                stb = st.astype(BF16)
                v_new = u[rows, hs] - jnp.dot(w[rows, hs].astype(BF16), stb,
                                              preferred_element_type=F32)
                vb16 = v_new.astype(BF16)
                o[rows, hs] = (jnp.dot(q[rows, hs].astype(BF16), stb, preferred_element_type=F32)
                               + jnp.dot(a[rows, h * n:(h + 1) * n].astype(BF16), vb16,
                                         preferred_element_type=F32))
                upd = lax.dot_general(k[rows, hs].astype(BF16), vb16, (((0,), (0,)), ((), ())),
                                      preferred_element_type=F32)
                gl = N_INST + inst
                state[inst] = st * dlrow[:, gl:gl + 1] + upd
        return carry

    lax.fori_loop(0, cb, chunk, 0)


def _delta_scan(u, w, qg, kd, a, dl, batch, seq, cb):
    t = u.shape[0]
    tt = cb * A_CHUNK
    nb = seq // tt
    hw = A_HEADS * A_DV
    fwd = lambda b, i: (b * nb + i, 0)
    bwd = lambda b, i: (b * nb + nb - 1 - i, 1)
    bwd0 = lambda b, i: (b * nb + nb - 1 - i, 0)
    big_f = pl.BlockSpec((tt, hw), fwd)
    big_b = pl.BlockSpec((tt, hw), bwd)
    aw = A_HEADS * A_CHUNK
    kern = functools.partial(_delta_scan_kernel, cb=cb)
    out = jax.ShapeDtypeStruct((t, hw), F32)
    return pl.pallas_call(
        kern, out_shape=(out, out), grid=(batch, nb),
        in_specs=[big_f] * 4 + [pl.BlockSpec((tt, aw), fwd), pl.BlockSpec((cb, LANE), fwd)]
        + [big_b] * 4 + [pl.BlockSpec((tt, aw), bwd), pl.BlockSpec((cb, LANE), bwd0)],
        out_specs=(pl.BlockSpec((tt, hw), fwd), pl.BlockSpec((tt, hw), bwd0)),
        scratch_shapes=[pltpu.VMEM((N_INST, A_DK, A_DV), F32)],
        compiler_params=_params(("parallel", "arbitrary")), name="delta_scan",
    )(u, w, qg, kd, a, dl, u, w, qg, kd, a, dl)


def _outproj_kernel(of_ref, ob_ref, z_ref, wb_ref, dc_ref, ga_ref, wo_ref, x_ref, g2_ref, wr_ref,
                    x1_ref, h2_ref, aff_ref):
    o = of_ref[...] + ob_ref[...]
    z = z_ref[...]
    parts = []
    for h in range(A_HEADS):
        hs = slice(h * A_DV, (h + 1) * A_DV)
        oh = o[:, hs]
        ms = jnp.mean(oh * oh, axis=1, keepdims=True)
        zh = z[:, hs]
        parts.append((oh * lax.rsqrt(ms + EPS) * ga_ref[...] * (zh / (1.0 + jnp.exp(-zh)))
                      ).astype(BF16))
    mixed = jnp.concatenate(parts + [wb_ref[...], dc_ref[...]], axis=1)
    x1 = x_ref[...] + jnp.dot(mixed, wo_ref[...], preferred_element_type=F32)
    x1_ref[...] = x1
    ms = jnp.mean(x1 * x1, axis=1, keepdims=True)
    h2 = x1 * lax.rsqrt(ms + EPS) * g2_ref[...]
    h2_ref[...] = h2.astype(BF16)
    logits = lax.dot_general(wr_ref[...], h2, (((1,), (1,)), ((), ())),
                             preferred_element_type=F32, precision=HI)
    m = jnp.max(logits, axis=0, keepdims=True)
    p = jnp.exp(logits - m)
    aff_ref[0] = p / jnp.sum(p, axis=0, keepdims=True)


def _outproj(o_f, o_b, pa, wnd, dns, ga, w_out, x2, g2, wr_t, batch, seq, tm):
    t = x2.shape[0]
    nt = seq // tm
    row = lambda i: (i, 0)
    fixed = lambda i: (0, 0)
    hw = A_HEADS * A_DV
    aw = N_Q_HEADS * HD
    return pl.pallas_call(
        _outproj_kernel,
        out_shape=(jax.ShapeDtypeStruct((t, D_MODEL), F32),
                   jax.ShapeDtypeStruct((t, D_MODEL), BF16),
                   jax.ShapeDtypeStruct((batch, N_EXPERTS, seq), F32)),
        grid=(t // tm,),
        in_specs=[pl.BlockSpec((tm, hw), row), pl.BlockSpec((tm, hw), row),
                  pl.BlockSpec((tm, hw), lambda i: (i, QKV_A // hw)),
                  pl.BlockSpec((tm, aw), row), pl.BlockSpec((tm, aw), row),
                  pl.BlockSpec((1, A_DV), fixed),
                  pl.BlockSpec((D_MODEL, D_MODEL), fixed),
                  pl.BlockSpec((tm, D_MODEL), row),
                  pl.BlockSpec((1, D_MODEL), fixed),
                  pl.BlockSpec((N_EXPERTS, D_MODEL), fixed)],
        out_specs=(pl.BlockSpec((tm, D_MODEL), row), pl.BlockSpec((tm, D_MODEL), row),
                   pl.BlockSpec((1, N_EXPERTS, tm), lambda i: (i // nt, 0, i % nt))),
        compiler_params=_params(("parallel",)), name="outproj_router",
    )(o_f, o_b, pa, wnd, dns, ga, w_out, x2, g2, wr_t)


def _excl_cumsum(mask01, tri, ones, blk_strict):
    mb = mask01.astype(BF16)
    incl = jnp.dot(mb, tri, preferred_element_type=F32)
    tot = jnp.dot(mb, ones, preferred_element_type=F32)
    rowoff = jnp.dot(blk_strict, tot.astype(BF16), preferred_element_type=F32)
    return rowoff + incl - mask01, rowoff


def _select_kernel(aff_ref, pos_ref, gate_ref, off_ref, *, nr, cap):
    aff = aff_ref[0]
    rows = aff.shape[0]
    aff3 = aff.reshape(N_EXPERTS, nr, LANE)

    def count(mask):
        c = jnp.sum(jnp.where(mask, 1.0, 0.0), axis=2, keepdims=True)
        return jnp.sum(c, axis=1, keepdims=True)

    def as_float(word):
        return lax.bitcast_convert_type(word, F32)

    def search(i, thr):
        cand = thr | lax.shift_left(jnp.int32(1), 30 - i)
        return jnp.where(count(aff3 >= as_float(cand)) >= cap, cand, thr)

    thr = lax.fori_loop(0, 31, search, jnp.zeros((N_EXPERTS, 1, 1), jnp.int32))
    gt = aff3 >= as_float(thr + 1)
    eq = (aff3 >= as_float(thr)) & jnp.logical_not(gt)
    need = cap - count(gt)

    r = lax.broadcasted_iota(jnp.int32, (LANE, LANE), 0)
    c = lax.broadcasted_iota(jnp.int32, (LANE, LANE), 1)
    tri = jnp.where(r <= c, 1.0, 0.0).astype(BF16)
    ones = jnp.ones((LANE, LANE), BF16)
    rr = lax.broadcasted_iota(jnp.int32, (rows, rows), 0)
    cc = lax.broadcasted_iota(jnp.int32, (rows, rows), 1)
    blk_strict = jnp.where((rr // nr == cc // nr) & (cc < rr), 1.0, 0.0).astype(BF16)

    eq01 = jnp.where(eq, 1.0, 0.0).reshape(rows, LANE)
    eq_rank, _ = _excl_cumsum(eq01, tri, ones, blk_strict)
    sel = gt | (eq & (eq_rank.reshape(N_EXPERTS, nr, LANE) < need))
    sel01 = jnp.where(sel, 1.0, 0.0).reshape(rows, LANE)
    pos, rowoff = _excl_cumsum(sel01, tri, ones, blk_strict)
    pos_ref[0] = jnp.where(sel01 > 0.0, pos, -1.0).astype(jnp.int32)
    gate_ref[0] = jnp.where(sel01 > 0.0, aff, 0.0)
    off_ref[0] = rowoff.astype(jnp.int32)


def _select(aff_t, batch, seq):
    nr = seq // LANE
    rows = N_EXPERTS * nr
    cap = CAP_FACTOR * seq // N_EXPERTS
    aff3 = aff_t.reshape(batch, rows, LANE)
    spec = pl.BlockSpec((1, rows, LANE), lambda b: (b, 0, 0))
    kern = functools.partial(_select_kernel, nr=nr, cap=cap)
    return pl.pallas_call(
        kern,
        out_shape=(jax.ShapeDtypeStruct((batch, rows, LANE), jnp.int32),
                   jax.ShapeDtypeStruct((batch, rows, LANE), F32),
                   jax.ShapeDtypeStruct((batch, rows, LANE), jnp.int32)),
        grid=(batch,), in_specs=[spec], out_specs=(spec, spec, spec),
        compiler_params=_params(("parallel",)), name="expert_select",
    )(aff3)


TOK_BLK = 2 * LANE
WIN = 64


def _tok_lanes(ref, b0, row):
    two = ref[b0, pl.ds(row, 2), :]
    return jnp.concatenate([two[0:1], two[1:2]], axis=1)


def _moe_up_kernel(boff_ref, h2_ref, pos_ref, gate_ref, wg_ref, wu_ref, hid_ref, xs, gr,
                   *, nblk, cap, ftile):
    b = pl.program_id(0)
    e = pl.program_id(1)
    base = (b * N_EXPERTS + e) * (nblk + 1)
    xs[...] = jnp.zeros_like(xs)
    gr[...] = jnp.zeros_like(gr)
    riota = lax.broadcasted_iota(jnp.int32, (WIN, TOK_BLK), 0)

    def block(j, carry):
        off = boff_ref[base + j]
        cnt = boff_ref[base + j + 1] - off
        r0 = (off // 8) * 8
        nwin = jnp.where(cnt > 0, (off + cnt - r0 + WIN - 1) // WIN, 0)
        prow = _tok_lanes(pos_ref, 0, 2 * j)
        grow = _tok_lanes(gate_ref, 0, 2 * j)
        toks = h2_ref[0, pl.ds(pl.multiple_of(j * TOK_BLK, TOK_BLK), TOK_BLK), :]

        def window(k, c2):
            rb = pl.multiple_of(r0 + k * WIN, 8)
            hit = (riota + rb) == prow
            onehot = jnp.where(hit, 1.0, 0.0).astype(BF16)
            xs[pl.ds(rb, WIN), :] += jnp.dot(onehot, toks, preferred_element_type=F32)
            g = jnp.sum(jnp.where(hit, grow, 0.0), axis=1, keepdims=True)
            gr[pl.ds(rb, WIN), :] += jnp.broadcast_to(g, (WIN, LANE))
            return c2

        lax.fori_loop(0, nwin, window, 0)
        return carry

    lax.fori_loop(0, nblk, block, 0)
    xsb = xs[0:cap, :].astype(BF16)
    gate = jnp.concatenate([gr[0:cap, :]] * (ftile // LANE), axis=1)
    for f in range(D_EXPERT // ftile):
        fs = slice(f * ftile, (f + 1) * ftile)
        g = jnp.dot(xsb, wg_ref[0, 0, :, fs].astype(BF16), preferred_element_type=F32)
        u = jnp.dot(xsb, wu_ref[0, 0, :, fs].astype(BF16), preferred_element_type=F32)
        hid_ref[0, 0, :, fs] = (g / (1.0 + jnp.exp(-g)) * u * gate).astype(BF16)


def _moe_up(boff, h2, pos, gate, wg, wu, layer, batch, seq):
    nr = seq // LANE
    nblk = seq // TOK_BLK
    cap = CAP_FACTOR * seq // N_EXPERTS
    kern = functools.partial(_moe_up_kernel, nblk=nblk, cap=cap, ftile=512)
    grid_spec = pltpu.PrefetchScalarGridSpec(
        num_scalar_prefetch=1, grid=(batch, N_EXPERTS),
        in_specs=[pl.BlockSpec((1, seq, D_MODEL), lambda b, e, s: (b, 0, 0),
                               pipeline_mode=pl.Buffered(1)),
                  pl.BlockSpec((1, nr, LANE), lambda b, e, s: (b, e, 0)),
                  pl.BlockSpec((1, nr, LANE), lambda b, e, s: (b, e, 0)),
                  pl.BlockSpec((1, 1, D_MODEL, D_EXPERT), lambda b, e, s: (layer, e, 0, 0)),
                  pl.BlockSpec((1, 1, D_MODEL, D_EXPERT), lambda b, e, s: (layer, e, 0, 0))],
        out_specs=pl.BlockSpec((1, 1, cap, D_EXPERT), lambda b, e, s: (b, e, 0, 0)),
        scratch_shapes=[pltpu.VMEM((cap + WIN, D_MODEL), F32),
                        pltpu.VMEM((cap + WIN, LANE), F32)])
    return pl.pallas_call(
        kern, out_shape=jax.ShapeDtypeStruct((batch, N_EXPERTS, cap, D_EXPERT), BF16),
        grid_spec=grid_spec,
        compiler_params=_params(("arbitrary", "arbitrary")), name="moe_gather_up",
    )(boff, h2, pos, gate, wg, wu)


GROUP = 4


def _moe_down_kernel(boff_ref, hid_ref, wd_ref, pos_ref, x1_ref, out_ref, ys,
                     *, nblk, nr, cap):
    b = pl.program_id(0)
    e = pl.program_id(2)

    @pl.when(e == 0)
    def _():
        ys[:, cap:, :] = jnp.zeros((N_EXPERTS, WIN, ys.shape[2]), ys.dtype)

    ys[e, 0:cap, :] = jnp.dot(hid_ref[0, 0], wd_ref[0, 0].astype(BF16), preferred_element_type=F32
                              ).astype(ys.dtype)

    @pl.when(e == N_EXPERTS - 1)
    def _():
        riota = lax.broadcasted_iota(jnp.int32, (WIN, TOK_BLK), 0)

        def block(j, carry):
            rows = pl.ds(pl.multiple_of(j * TOK_BLK, TOK_BLK), TOK_BLK)
            acc = x1_ref[0, rows, :]
            for g0 in range(0, N_EXPERTS, GROUP):
                starts, prows, npass = [], [], 0
                for e2 in range(g0, g0 + GROUP):
                    base = (b * N_EXPERTS + e2) * (nblk + 1)
                    off = boff_ref[base + j]
                    cnt = boff_ref[base + j + 1] - off
                    r0 = (off // 16) * 16
                    nwin = jnp.where(cnt > 0, (off + cnt - r0 + WIN - 1) // WIN, 0)
                    npass = jnp.maximum(npass, nwin)
                    starts.append(r0)
                    prows.append(_tok_lanes(pos_ref, 0, e2 * nr + 2 * j))

                def one_pass(k, a):
                    hots, vals = [], []
                    for i2, e2 in enumerate(range(g0, g0 + GROUP)):
                        rb = pl.multiple_of(jnp.minimum(starts[i2] + k * WIN, cap), 16)
                        hots.append(jnp.where((riota + rb) == prows[i2], 1.0, 0.0).astype(BF16))
                        vals.append(ys[e2, pl.ds(rb, WIN), :])
                    return a + lax.dot_general(
                        jnp.concatenate(hots, axis=0), jnp.concatenate(vals, axis=0),
                        (((0,), (0,)), ((), ())), preferred_element_type=F32)

                acc = lax.fori_loop(0, npass, one_pass, acc)
            out_ref[0, rows, :] = acc
            return carry

        lax.fori_loop(0, nblk, block, 0)


def _moe_down(boff, hid, wd, pos, x1, layer, batch, seq, td):
    nr = seq // LANE
    nblk = seq // TOK_BLK
    cap = CAP_FACTOR * seq // N_EXPERTS
    kern = functools.partial(_moe_down_kernel, nblk=nblk, nr=nr, cap=cap)
    grid_spec = pltpu.PrefetchScalarGridSpec(
        num_scalar_prefetch=1, grid=(batch, D_MODEL // td, N_EXPERTS),
        in_specs=[pl.BlockSpec((1, 1, cap, D_EXPERT), lambda b, d, e, s: (b, e, 0, 0)),
                  pl.BlockSpec((1, 1, D_EXPERT, td), lambda b, d, e, s: (layer, e, 0, d)),
                  pl.BlockSpec((1, N_EXPERTS * nr, LANE), lambda b, d, e, s: (b, 0, 0)),
                  pl.BlockSpec((1, seq, td), lambda b, d, e, s: (b, 0, d))],
        out_specs=pl.BlockSpec((1, seq, td), lambda b, d, e, s: (b, 0, d)),
        scratch_shapes=[pltpu.VMEM((N_EXPERTS, cap + WIN, td), BF16)])
    return pl.pallas_call(
        kern, out_shape=jax.ShapeDtypeStruct((batch, seq, D_MODEL), F32),
        grid_spec=grid_spec,
        compiler_params=_params(("arbitrary", "arbitrary", "arbitrary")),
        name="moe_down_scatter",
    )(boff, hid, wd, pos, x1)


def _block_offsets(rowoff, batch, seq):
    nr = seq // LANE
    cap = CAP_FACTOR * seq // N_EXPERTS
    per_row = rowoff[:, :, 0].reshape(batch, N_EXPERTS, nr)
    step = TOK_BLK // LANE
    offs = jnp.concatenate([per_row[:, :, ::step],
                            jnp.full((batch, N_EXPERTS, 1), cap, jnp.int32)], axis=2)
    return offs.reshape(-1)


def _delta_params(conv_w, a_log, dt_bias):
    cw8 = jnp.concatenate([conv_w.astype(F32),
                           jnp.zeros((HALO - A_CONV, conv_w.shape[1]), F32)], axis=0)
    pad_l = jnp.zeros((N_INST,), F32)
    pad_r = jnp.zeros((LANE - 2 * N_INST,), F32)
    aneg = jnp.concatenate([pad_l, -jnp.exp(a_log.astype(F32)).reshape(-1), pad_r])[None, :]
    dtb = jnp.concatenate([pad_l, dt_bias.astype(F32).reshape(-1), pad_r])[None, :]
    return cw8, aneg, dtb


def _reorder_w_in(w):
    d = w.shape[0]
    gates = w[:, PA_W:PA_W + 4 * A_HEADS]
    rest = w[:, PA_W + 4 * A_HEADS:]
    pad = jnp.zeros((d, PROJ_W - OFF_G - 4 * A_HEADS), w.dtype)
    return jnp.concatenate([w[:, :PA_W], rest, gates, pad], axis=1)


def _qk_gains(gq, gk):
    scale = (HD ** -0.5) * math.log2(math.e)
    return jnp.concatenate([jnp.tile(gq.astype(F32) * scale, N_Q_HEADS),
                            jnp.tile(gk.astype(F32), N_KV_HEADS)])[None, :]


def _tiles(seq):
    tm = min(512, seq)
    return dict(tm=tm, tq=min(256, seq), tq_dense=min(256, seq), tk_dense=min(2048, seq),
                tk_win=min(256 + 2 * WINDOW, seq),
                tt=tm, cb=tm // A_CHUNK, td=256)


def _layer(x, p, l, tabs, batch, seq):
    ts = _tiles(seq)
    x2 = x.reshape(batch * seq, D_MODEL)
    w_re = _reorder_w_in(p["w_in"][l]).astype(BF16)
    gain_b = _qk_gains(p["b_qnorm_g"][l], p["b_knorm_g"][l])
    gain_c = _qk_gains(p["c_qnorm_g"][l], p["c_knorm_g"][l])
    pa, gb, qb, kb, vb, qc, kc, vc = _inproj(
        x2, p["norm1_g"][l].astype(F32)[None, :], w_re, gain_b, gain_c, tabs, seq, ts["tm"])

    cw8, aneg, dtb = _delta_params(p["conv_w"][l], p["a_log"][l], p["dt_bias"][l])
    u, w, qg, kd, a, dl = _delta_prep(pa, gb, cw8, aneg, dtb, seq, ts["tt"])
    o_f, o_b = _delta_scan(u, w, qg, kd, a, dl, batch, seq, ts["cb"])

    sinks = p["b_sink"][l].astype(F32) * math.log2(math.e)
    wnd = _attention(qb, kb, vb, sinks, batch, seq, ts["tq"], ts["tk_win"], True)
    dns = _attention(qc, kc, vc, sinks, batch, seq, ts["tq_dense"], ts["tk_dense"], False)

    x1, h2, aff_t = _outproj(
        o_f, o_b, pa, wnd, dns, p["a_norm_g"][l].astype(F32)[None, :],
        p["w_out"][l].astype(BF16), x2, p["norm2_g"][l].astype(F32)[None, :],
        p["w_router"][l].astype(F32).T, batch, seq, ts["tm"])

    pos, gate, rowoff = _select(aff_t, batch, seq)
    boff = _block_offsets(rowoff, batch, seq)
    hid = _moe_up(boff, h2.reshape(batch, seq, D_MODEL), pos, gate,
                  p["w_gate"], p["w_up"], l, batch, seq)
    return _moe_down(boff, hid, p["w_down"], pos,
                     x1.reshape(batch, seq, D_MODEL), l, batch, seq, ts["td"])


def kernel(x, norm1_g, w_in, conv_w, a_log, dt_bias, a_norm_g, b_qnorm_g, b_knorm_g, b_sink,
           c_qnorm_g, c_knorm_g, w_out, norm2_g, w_router, w_gate, w_up, w_down):
    batch, seq, _ = x.shape
    p = dict(norm1_g=norm1_g, w_in=w_in, conv_w=conv_w, a_log=a_log, dt_bias=dt_bias,
             a_norm_g=a_norm_g, b_qnorm_g=b_qnorm_g, b_knorm_g=b_knorm_g, b_sink=b_sink,
             c_qnorm_g=c_qnorm_g, c_knorm_g=c_knorm_g, w_out=w_out, norm2_g=norm2_g,
             w_router=w_router, w_gate=w_gate, w_up=w_up, w_down=w_down)
    tabs = _rope_tables(seq)
    for l in range(w_in.shape[0]):
        x = _layer(x, p, l, tabs, batch, seq)
    return x
```

```python
import functools
import math

import jax
import jax.numpy as jnp
from jax import lax
from jax.experimental import pallas as pl
from jax.experimental.pallas import tpu as pltpu

F32 = jnp.float32
BF16 = jnp.bfloat16

D_MODEL = 1024
GRID_W = 64
EPS = 1e-6
ROPE_THETA = 10000.0
A_HEADS = 4
A_DK = 128
A_DV = 128
A_CONV = 5
A_CHUNK = 64
HD = 64
N_Q_HEADS = 4
N_KV_HEADS = 2
WINDOW = 128
N_EXPERTS = 16
CAP_FACTOR = 2
D_EXPERT = 1024

QKV_A = 3 * A_HEADS * A_DK
PA_W = QKV_A + A_HEADS * A_DV
ATT_W = (N_Q_HEADS + 2 * N_KV_HEADS) * HD
QK_W = (N_Q_HEADS + N_KV_HEADS) * HD
OFF_B = PA_W
OFF_C = PA_W + ATT_W
OFF_G = PA_W + 2 * ATT_W
LANE = 128
PROJ_W = OFF_G + LANE

VMEM_LIMIT = 56 * 1024 * 1024


def _params(sem, vmem=None):
    return pltpu.CompilerParams(dimension_semantics=sem,
                                vmem_limit_bytes=vmem or VMEM_LIMIT)


def _group_sumsq(v, width):
    n = v.shape[-1]
    r = lax.broadcasted_iota(jnp.int32, (n, n), 0) // width
    c = lax.broadcasted_iota(jnp.int32, (n, n), 1) // width
    ones = jnp.where(r == c, 1.0, 0.0).astype(BF16)
    sq = v * v
    hi = sq.astype(BF16)
    lo = (sq - hi.astype(F32)).astype(BF16)
    return (jnp.dot(hi, ones, preferred_element_type=F32)
            + jnp.dot(lo, ones, preferred_element_type=F32))


def _dup_halves(v):
    lane = lax.broadcasted_iota(jnp.int32, v.shape, 1)
    sw = pltpu.roll(v, 64, 1)
    lo = lane < 64
    return jnp.concatenate([jnp.where(lo, v, sw), jnp.where(lo, sw, v)], axis=1)


def _with_ones(v):
    lane = lax.broadcasted_iota(jnp.int32, v.shape, 1)
    lo = lane < 64
    return jnp.concatenate([jnp.where(lo, v, 1.0),
                            jnp.where(lo, pltpu.roll(v, 64, 1), 1.0)], axis=1)


def _inproj_kernel(x_ref, g1_ref, w_ref, gain_b_ref, gain_c_ref,
                   cosb_ref, sinb_ref, cosc_ref, sinc_ref,
                   pa_ref, gb_ref, qb_ref, kb_ref, vb_ref, qc_ref, kc_ref, vc_ref):
    x = x_ref[...]
    ms = jnp.mean(x * x, axis=-1, keepdims=True)
    h = (x * lax.rsqrt(ms + EPS) * g1_ref[...]).astype(BF16)
    pa_ref[...] = jnp.dot(h, w_ref[:, 0:PA_W], preferred_element_type=F32)
    gb_ref[...] = jnp.dot(h, w_ref[:, OFF_G:PROJ_W], preferred_element_type=F32)

    def softmax_mixer(off, gain_ref, cos_ref, sin_ref, half, q_ref, k_ref, v_ref):
        y = jnp.dot(h, w_ref[:, off:off + ATT_W], preferred_element_type=F32)
        qk = y[:, :QK_W]
        ss = _group_sumsq(qk, HD)
        n = qk * lax.rsqrt(ss * (1.0 / HD) + EPS) * gain_ref[...]
        cos = jnp.concatenate([cos_ref[...]] * (QK_W // LANE), axis=1)
        sin = jnp.concatenate([sin_ref[...]] * (QK_W // LANE), axis=1)
        lane = lax.broadcasted_iota(jnp.int32, n.shape, 1)
        first = (lane % (2 * half)) < half
        rot = jnp.where(first, pltpu.roll(n, QK_W - half, 1), pltpu.roll(n, half, 1))
        r = n * cos + rot * sin
        q_ref[...] = r[:, :N_Q_HEADS * HD].astype(BF16)
        k_ref[...] = _dup_halves(r[:, N_Q_HEADS * HD:QK_W]).astype(BF16)
        v_ref[...] = _with_ones(y[:, QK_W:ATT_W]).astype(BF16)

    softmax_mixer(OFF_B, gain_b_ref, cosb_ref, sinb_ref, HD // 2, qb_ref, kb_ref, vb_ref)
    softmax_mixer(OFF_C, gain_c_ref, cosc_ref, sinc_ref, HD // 4, qc_ref, kc_ref, vc_ref)


def _inproj(x2, g1, w_re, gain_b, gain_c, tabs, seq, tm):
    t = x2.shape[0]
    nt = seq // tm
    row = lambda i: (i, 0)
    fixed = lambda i: (0, 0)
    tab = lambda i: (i % nt, 0)
    kv_w = 2 * N_KV_HEADS * HD
    out_shape = (
        jax.ShapeDtypeStruct((t, PA_W), F32),
        jax.ShapeDtypeStruct((t, LANE), F32),
        jax.ShapeDtypeStruct((t, N_Q_HEADS * HD), BF16),
        jax.ShapeDtypeStruct((t, kv_w), BF16),
        jax.ShapeDtypeStruct((t, kv_w), BF16),
        jax.ShapeDtypeStruct((t, N_Q_HEADS * HD), BF16),
        jax.ShapeDtypeStruct((t, kv_w), BF16),
        jax.ShapeDtypeStruct((t, kv_w), BF16),
    )
    in_specs = [
        pl.BlockSpec((tm, D_MODEL), row),
        pl.BlockSpec((1, D_MODEL), fixed),
        pl.BlockSpec((D_MODEL, PROJ_W), fixed),
        pl.BlockSpec((1, QK_W), fixed),
        pl.BlockSpec((1, QK_W), fixed),
    ] + [pl.BlockSpec((tm, LANE), tab)] * 4
    out_specs = tuple(pl.BlockSpec((tm, s.shape[1]), row) for s in out_shape)
    return pl.pallas_call(
        _inproj_kernel, out_shape=out_shape, grid=(t // tm,),
        in_specs=in_specs, out_specs=out_specs,
        compiler_params=_params(("parallel",)), name="inproj",
    )(x2, g1, w_re, gain_b, gain_c, *tabs)


def _rope_tables(seq):
    def tables(pos, dim):
        inv = ROPE_THETA ** (-jnp.arange(0, dim, 2, dtype=F32) / dim)
        ang = pos.astype(F32)[:, None] * inv[None, :]
        cos = jnp.concatenate([jnp.cos(ang), jnp.cos(ang)], axis=-1)
        sin = jnp.concatenate([-jnp.sin(ang), jnp.sin(ang)], axis=-1)
        return cos, sin
    pos = jnp.arange(seq)
    cos1, sin1 = tables(pos, HD)
    cosr, sinr = tables(pos // GRID_W, HD // 2)
    cosc, sinc = tables(pos % GRID_W, HD // 2)
    cos2 = jnp.concatenate([cosr, cosc], axis=-1)
    sin2 = jnp.concatenate([sinr, sinc], axis=-1)
    two = lambda a: jnp.concatenate([a, a], axis=-1)
    return two(cos1), two(sin1), two(cos2), two(sin2)


NEG_BIG = -1e30


def _attn_kernel(sink_ref, q_ref, k_ref, v_ref, o_ref, *, tq, tk, seq, window):
    qi = pl.program_id(1)
    q0 = qi * tq
    lane = lax.broadcasted_iota(jnp.int32, (tq, LANE), 1)
    row2 = lax.broadcasted_iota(jnp.int32, (2 * tq, 1), 0)
    if window:
        qpos = q0 + row2 % tq
    groups = range(N_KV_HEADS)
    cols = [slice(g * LANE, (g + 1) * LANE) for g in groups]
    q2s, init = [], []
    for g in groups:
        qg = q_ref[:, cols[g]]
        zero = jnp.zeros_like(qg)
        q2s.append(jnp.concatenate([jnp.where(lane < HD, qg, zero),
                                    jnp.where(lane >= HD, qg, zero)], axis=0))
        lane2 = lax.broadcasted_iota(jnp.int32, (2 * tq, LANE), 1)
        if window:
            m0 = jnp.where(row2 < tq, sink_ref[2 * g], sink_ref[2 * g + 1])
            a0 = jnp.where(lane2 >= HD, 1.0, 0.0)
        else:
            m0 = jnp.full((2 * tq, 1), NEG_BIG, F32)
            a0 = jnp.zeros((2 * tq, LANE), F32)
        init.append((m0, a0))

    def step(k0, carry):
        out = []
        for g in groups:
            m, acc = carry[g]
            kc = k_ref[pl.ds(k0, tk), cols[g]]
            vc = v_ref[pl.ds(k0, tk), cols[g]]
            s = lax.dot_general(q2s[g], kc, (((1,), (1,)), ((), ())),
                                preferred_element_type=F32)
            if window:
                kpos = k0 + lax.broadcasted_iota(jnp.int32, (1, tk), 1)
                s = jnp.where(jnp.abs(qpos - kpos) <= WINDOW, s, NEG_BIG)
            m_new = jnp.maximum(m, jnp.max(s, axis=1, keepdims=True))
            alpha = jnp.exp2(m - m_new)
            p = jnp.exp2((s - m_new).astype(BF16))
            acc = alpha * acc + jnp.dot(p, vc, preferred_element_type=F32)
            out.append((m_new, acc))
        return tuple(out)

    if window:
        res = step(pl.multiple_of(jnp.clip(q0 - WINDOW, 0, seq - tk), LANE), tuple(init))
    else:
        res = lax.fori_loop(0, seq // tk,
                            lambda c, carry: step(pl.multiple_of(c * tk, tk), carry),
                            tuple(init))
    for g in groups:
        acc = res[g][1]
        o = acc / pltpu.roll(acc, HD, 1)
        o_ref[:, cols[g]] = jnp.where(lane < HD, o[:tq],
                                      pltpu.roll(o[tq:], HD, 1)).astype(o_ref.dtype)


def _attention(q, kd, vd, sinks, batch, seq, tq, tk, window):
    t = q.shape[0]
    nq = seq // tq
    width = N_Q_HEADS * HD
    kern = functools.partial(_attn_kernel, tq=tq, tk=tk, seq=seq, window=window)
    return pl.pallas_call(
        kern, out_shape=jax.ShapeDtypeStruct((t, width), BF16), grid=(batch, nq),
        in_specs=[pl.BlockSpec(memory_space=pltpu.SMEM),
                  pl.BlockSpec((tq, width), lambda b, i: (b * nq + i, 0)),
                  pl.BlockSpec((seq, width), lambda b, i: (b, 0)),
                  pl.BlockSpec((seq, width), lambda b, i: (b, 0))],
        out_specs=pl.BlockSpec((tq, width), lambda b, i: (b * nq + i, 0)),
        compiler_params=_params(("parallel", "parallel")),
        name="window_attention" if window else "dense_attention",
    )(sinks, q, kd, vd)


N_INST = 2 * A_HEADS
HALO = 8
HI = lax.Precision.HIGHEST


def _dot_hi(a, b):
    return jnp.dot(a, b, preferred_element_type=F32, precision=HI)


def _dot_bf(a, b):
    return jnp.dot(a.astype(BF16), b.astype(BF16), preferred_element_type=F32)


def _block_diag(m, reps, blk_r, blk_c):
    tall = jnp.concatenate([m] * reps, axis=0)
    r = lax.broadcasted_iota(jnp.int32, tall.shape, 0) // blk_r
    c = lax.broadcasted_iota(jnp.int32, tall.shape, 1) // blk_c
    return jnp.where(r == c, tall, jnp.zeros_like(tall))


def _unit_tri_inverse_stacked(low, reps):
    n = low.shape[0]
    r = lax.broadcasted_iota(jnp.int32, low.shape, 0)
    c = lax.broadcasted_iota(jnp.int32, low.shape, 1) % n
    p = -low
    x = jnp.where(r == c, 1.0, 0.0) + p
    for _ in range(int(math.log2(n)) - 1):
        p = jnp.dot(p.astype(BF16), _block_diag(p.astype(BF16), reps, n, n),
                    preferred_element_type=F32)
        x = x + jnp.dot(x.astype(BF16), _block_diag(p.astype(BF16), reps, n, n),
                        preferred_element_type=F32)
    return x


def _lane_blocks(pieces, n, width):
    total = len(pieces) * width
    lane = lax.broadcasted_iota(jnp.int32, (n, total), 1) // width
    out = jnp.broadcast_to(pieces[-1], (n, total))
    for j in range(len(pieces) - 2, -1, -1):
        out = jnp.where(lane == j, jnp.broadcast_to(pieces[j], (n, total)), out)
    return out


def _delta_prep_kernel(prev_ref, cur_ref, next_ref, gb_ref, cw_ref, aneg_ref, dtb_ref,
                       u_ref, w_ref, qg_ref, kd_ref, a_ref, dl_ref,
                       qs, ks, vs, gs, bs, ext, *, tt, tiles_per_seq):
    i = pl.program_id(0) % tiles_per_seq
    ext[0:HALO, :] = jnp.where(i == 0, 0.0, prev_ref[...])
    ext[HALO:HALO + tt, :] = cur_ref[...]
    ext[HALO + tt:, :] = jnp.where(i == tiles_per_seq - 1, 0.0, next_ref[...])
    y = jnp.zeros((tt, QKV_A), F32)
    for j in range(A_CONV):
        s = HALO - A_CONV // 2 + j
        y = y + ext[s:s + tt, :] * cw_ref[j:j + 1, :]
    y = y / (1.0 + jnp.exp(-y))
    hw = A_HEADS * A_DK
    for h in range(A_HEADS):
        for src, dst, scale in ((0, qs, A_DK ** -0.5), (hw, ks, 1.0)):
            xh = y[:, src + h * A_DK: src + (h + 1) * A_DK]
            ss = jnp.sum(xh * xh, axis=1, keepdims=True)
            dst[:, h * A_DK:(h + 1) * A_DK] = xh * (lax.rsqrt(ss + EPS) * scale)
    vs[...] = y[:, 2 * hw:]
    gbv = gb_ref[...]
    bs[...] = 1.0 / (1.0 + jnp.exp(-gbv))
    xg = gbv + dtb_ref[...]
    softplus = jnp.maximum(xg, 0.0) + jnp.log1p(jnp.exp(-jnp.abs(xg)))
    gs[...] = aneg_ref[...] * softplus

    n = A_CHUNK
    nh = A_HEADS
    r = lax.broadcasted_iota(jnp.int32, (n, n), 0)
    c = lax.broadcasted_iota(jnp.int32, (n, n), 1)
    r4 = lax.broadcasted_iota(jnp.int32, (n, nh * n), 0)
    c4 = lax.broadcasted_iota(jnp.int32, (n, nh * n), 1) % n
    rt = lax.broadcasted_iota(jnp.int32, (nh * n, n), 0) % n
    ct = lax.broadcasted_iota(jnp.int32, (nh * n, n), 1)

    def chunk(ci, carry):
        r0 = pl.multiple_of(ci * n, n)
        rows = pl.ds(r0, n)
        g8 = gs[rows, :]
        b8 = bs[rows, :]
        q_all = qs[rows, :]
        k_all = ks[rows, :]
        v_all = vs[rows, :]
        gtot = jnp.sum(g8, axis=0, keepdims=True)
        dl_ref[pl.ds(ci, 1), :] = jnp.exp(gtot)
        k_bd = _block_diag(k_all.astype(BF16), nh, n, A_DK)
        for d in range(2):
            fwd = d == 0
            tri = jnp.where((c <= r) if fwd else (c >= r), 1.0, 0.0)
            tri_t4 = jnp.where((ct <= rt) if fwd else (ct >= rt), 1.0, 0.0)
            keep = (c4 <= r4) if fwd else (c4 >= r4)
            strict = (c4 < r4) if fwd else (c4 > r4)
            gc_all = _dot_hi(tri, g8)
            gc_row = lax.dot_general(g8, tri_t4, (((0,), (1,)), ((), ())),
                                     preferred_element_type=F32, precision=HI)
            lanes = [N_INST + d * nh + h for h in range(nh)]
            gcs = [gc_all[:, gl:gl + 1] for gl in lanes]
            egs = [jnp.exp(gc) for gc in gcs]
            betas = [b8[:, d * nh + h:d * nh + h + 1] for h in range(nh)]
            diff = (_lane_blocks(gcs, n, n)
                    - _lane_blocks([gc_row[gl:gl + 1, :] for gl in lanes], n, n))
            decay = jnp.where(keep, jnp.exp(jnp.where(keep, diff, 0.0)), 0.0)
            beta_w = _lane_blocks(betas, n, A_DK)
            eg_w = _lane_blocks(egs, n, A_DK)
            kb_all = k_all * beta_w
            both = lax.dot_general(jnp.concatenate([kb_all, q_all], axis=0).astype(BF16),
                                   k_bd, (((1,), (1,)), ((), ())),
                                   preferred_element_type=F32)
            tm = _unit_tri_inverse_stacked(jnp.where(strict, both[:n] * decay, 0.0), nh)
            vb_all = v_all * beta_w
            kbe_all = kb_all * eg_w
            rhs = jnp.concatenate(
                [jnp.concatenate([vb_all[:, h * A_DK:(h + 1) * A_DK],
                                  kbe_all[:, h * A_DK:(h + 1) * A_DK]], axis=1)
                 for h in range(nh)], axis=0)
            uw = jnp.dot(_block_diag(tm.astype(BF16), nh, n, n), rhs.astype(BF16),
                         preferred_element_type=F32)
            wide = slice(d * nh * A_DK, (d + 1) * nh * A_DK)
            for h in range(nh):
                o128 = slice((d * nh + h) * A_DK, (d * nh + h + 1) * A_DK)
                u_ref[rows, o128] = uw[h * n:(h + 1) * n, :A_DV]
                w_ref[rows, o128] = uw[h * n:(h + 1) * n, A_DV:]
            qg_ref[rows, wide] = q_all * eg_w
            kd_ref[rows, wide] = k_all * jnp.exp(
                _lane_blocks([gtot[:, gl:gl + 1] - gc for gl, gc in zip(lanes, gcs)], n, A_DK))
            a_ref[rows, d * nh * n:(d + 1) * nh * n] = both[n:] * decay
        return carry

    lax.fori_loop(0, tt // n, chunk, 0)


def _delta_prep(pa, gb, conv_w8, aneg, dtb, seq, tt):
    t = pa.shape[0]
    tps = seq // tt
    hb = tt // HALO
    last = t // HALO - 1
    wide = N_INST * A_DK
    kern = functools.partial(_delta_prep_kernel, tt=tt, tiles_per_seq=tps)
    big = jax.ShapeDtypeStruct((t, wide), F32)
    out_shape = (big, big, big, big,
                 jax.ShapeDtypeStruct((t, N_INST * A_CHUNK), F32),
                 jax.ShapeDtypeStruct((t // A_CHUNK, LANE), F32))
    row = lambda i: (i, 0)
    fixed = lambda i: (0, 0)
    return pl.pallas_call(
        kern, out_shape=out_shape, grid=(t // tt,),
        in_specs=[pl.BlockSpec((HALO, QKV_A), lambda i: (jnp.maximum(i * hb - 1, 0), 0)),
                  pl.BlockSpec((tt, QKV_A), row),
                  pl.BlockSpec((HALO, QKV_A), lambda i: (jnp.minimum((i + 1) * hb, last), 0)),
                  pl.BlockSpec((tt, LANE), row),
                  pl.BlockSpec((HALO, QKV_A), fixed),
                  pl.BlockSpec((1, LANE), fixed),
                  pl.BlockSpec((1, LANE), fixed)],
        out_specs=(pl.BlockSpec((tt, wide), row),) * 4
        + (pl.BlockSpec((tt, N_INST * A_CHUNK), row),
           pl.BlockSpec((tt // A_CHUNK, LANE), row)),
        scratch_shapes=[pltpu.VMEM((tt, A_HEADS * A_DK), F32)] * 3
        + [pltpu.VMEM((tt, LANE), F32)] * 2 + [pltpu.VMEM((tt + 2 * HALO, QKV_A), F32)],
        compiler_params=_params(("parallel",)), name="delta_prep",
    )(pa, pa, pa, gb, conv_w8, aneg, dtb)


def _delta_scan_kernel(uf, wf, qf, kf, af, dlf, ub, wb, qb, kb, ab, dlb,
                       of_ref, ob_ref, state, *, cb):
    @pl.when(pl.program_id(1) == 0)
    def _():
        state[...] = jnp.zeros_like(state)

    n = A_CHUNK
    dirs = ((uf, wf, qf, kf, af, dlf, of_ref), (ub, wb, qb, kb, ab, dlb, ob_ref))

    def chunk(ci, carry):
        for d, (u, w, q, k, a, dl, o) in enumerate(dirs):
            cc = ci if d == 0 else cb - 1 - ci
            rows = pl.ds(pl.multiple_of(cc * n, n), n)
            dlrow = dl[pl.ds(cc, 1), :]
            for h in range(A_HEADS):
                inst = d * A_HEADS + h
                hs = slice(h * A_DK, (h + 1) * A_DK)
                st = state[inst]
                stb = st.astype(BF16)
                v_new = u[rows, hs] - jnp.dot(w[rows, hs].astype(BF16), stb,
                                              preferred_element_type=F32)
                vb16 = v_new.astype(BF16)
                o[rows, hs] = (jnp.dot(q[rows, hs].astype(BF16), stb, preferred_element_type=F32)
                               + jnp.dot(a[rows, h * n:(h + 1) * n].astype(BF16), vb16,
                                         preferred_element_type=F32))
                upd = lax.dot_general(k[rows, hs].astype(BF16), vb16, (((0,), (0,)), ((), ())),
                                      preferred_element_type=F32)
                gl = N_INST + inst
                state[inst] = st * dlrow[:, gl:gl + 1] + upd
        return carry

    lax.fori_loop(0, cb, chunk, 0)


def _delta_scan(u, w, qg, kd, a, dl, batch, seq, cb):
    t = u.shape[0]
    tt = cb * A_CHUNK
    nb = seq // tt
    hw = A_HEADS * A_DV
    fwd = lambda b, i: (b * nb + i, 0)
    bwd = lambda b, i: (b * nb + nb - 1 - i, 1)
    bwd0 = lambda b, i: (b * nb + nb - 1 - i, 0)
    big_f = pl.BlockSpec((tt, hw), fwd)
    big_b = pl.BlockSpec((tt, hw), bwd)
    aw = A_HEADS * A_CHUNK
    kern = functools.partial(_delta_scan_kernel, cb=cb)
    out = jax.ShapeDtypeStruct((t, hw), F32)
    return pl.pallas_call(
        kern, out_shape=(out, out), grid=(batch, nb),
        in_specs=[big_f] * 4 + [pl.BlockSpec((tt, aw), fwd), pl.BlockSpec((cb, LANE), fwd)]
        + [big_b] * 4 + [pl.BlockSpec((tt, aw), bwd), pl.BlockSpec((cb, LANE), bwd0)],
        out_specs=(pl.BlockSpec((tt, hw), fwd), pl.BlockSpec((tt, hw), bwd0)),
        scratch_shapes=[pltpu.VMEM((N_INST, A_DK, A_DV), F32)],
        compiler_params=_params(("parallel", "arbitrary")), name="delta_scan",
    )(u, w, qg, kd, a, dl, u, w, qg, kd, a, dl)


def _outproj_kernel(of_ref, ob_ref, z_ref, wb_ref, dc_ref, ga_ref, wo_ref, x_ref, g2_ref, wr_ref,
                    x1_ref, h2_ref, aff_ref):
    o = of_ref[...] + ob_ref[...]
    z = z_ref[...]
    parts = []
    for h in range(A_HEADS):
        hs = slice(h * A_DV, (h + 1) * A_DV)
        oh = o[:, hs]
        ms = jnp.mean(oh * oh, axis=1, keepdims=True)
        zh = z[:, hs]
        parts.append((oh * lax.rsqrt(ms + EPS) * ga_ref[...] * (zh / (1.0 + jnp.exp(-zh)))
                      ).astype(BF16))
    mixed = jnp.concatenate(parts + [wb_ref[...], dc_ref[...]], axis=1)
    x1 = x_ref[...] + jnp.dot(mixed, wo_ref[...], preferred_element_type=F32)
    x1_ref[...] = x1
    ms = jnp.mean(x1 * x1, axis=1, keepdims=True)
    h2 = x1 * lax.rsqrt(ms + EPS) * g2_ref[...]
    h2_ref[...] = h2.astype(BF16)
    logits = lax.dot_general(wr_ref[...], h2, (((1,), (1,)), ((), ())),
                             preferred_element_type=F32, precision=HI)
    m = jnp.max(logits, axis=0, keepdims=True)
    p = jnp.exp(logits - m)
    aff_ref[0] = p / jnp.sum(p, axis=0, keepdims=True)


def _outproj(o_f, o_b, pa, wnd, dns, ga, w_out, x2, g2, wr_t, batch, seq, tm):
    t = x2.shape[0]
    nt = seq // tm
    row = lambda i: (i, 0)
    fixed = lambda i: (0, 0)
    hw = A_HEADS * A_DV
    aw = N_Q_HEADS * HD
    return pl.pallas_call(
        _outproj_kernel,
        out_shape=(jax.ShapeDtypeStruct((t, D_MODEL), F32),
                   jax.ShapeDtypeStruct((t, D_MODEL), BF16),
                   jax.ShapeDtypeStruct((batch, N_EXPERTS, seq), F32)),
        grid=(t // tm,),
        in_specs=[pl.BlockSpec((tm, hw), row), pl.BlockSpec((tm, hw), row),
                  pl.BlockSpec((tm, hw), lambda i: (i, QKV_A // hw)),
                  pl.BlockSpec((tm, aw), row), pl.BlockSpec((tm, aw), row),
                  pl.BlockSpec((1, A_DV), fixed),
                  pl.BlockSpec((D_MODEL, D_MODEL), fixed),
                  pl.BlockSpec((tm, D_MODEL), row),
                  pl.BlockSpec((1, D_MODEL), fixed),
                  pl.BlockSpec((N_EXPERTS, D_MODEL), fixed)],
        out_specs=(pl.BlockSpec((tm, D_MODEL), row), pl.BlockSpec((tm, D_MODEL), row),
                   pl.BlockSpec((1, N_EXPERTS, tm), lambda i: (i // nt, 0, i % nt))),
        compiler_params=_params(("parallel",)), name="outproj_router",
    )(o_f, o_b, pa, wnd, dns, ga, w_out, x2, g2, wr_t)


def _excl_cumsum(mask01, tri, ones, blk_strict):
    mb = mask01.astype(BF16)
    incl = jnp.dot(mb, tri, preferred_element_type=F32)
    tot = jnp.dot(mb, ones, preferred_element_type=F32)
    rowoff = jnp.dot(blk_strict, tot.astype(BF16), preferred_element_type=F32)
    return rowoff + incl - mask01, rowoff


def _select_kernel(aff_ref, pos_ref, gate_ref, off_ref, *, nr, cap):
    aff = aff_ref[0]
    rows = aff.shape[0]
    aff3 = aff.reshape(N_EXPERTS, nr, LANE)

    def count(mask):
        c = jnp.sum(jnp.where(mask, 1.0, 0.0), axis=2, keepdims=True)
        return jnp.sum(c, axis=1, keepdims=True)

    def as_float(word):
        return lax.bitcast_convert_type(word, F32)

    def search(i, thr):
        cand = thr | lax.shift_left(jnp.int32(1), 30 - i)
        return jnp.where(count(aff3 >= as_float(cand)) >= cap, cand, thr)

    thr = lax.fori_loop(0, 31, search, jnp.zeros((N_EXPERTS, 1, 1), jnp.int32))
    gt = aff3 >= as_float(thr + 1)
    eq = (aff3 >= as_float(thr)) & jnp.logical_not(gt)
    need = cap - count(gt)

    r = lax.broadcasted_iota(jnp.int32, (LANE, LANE), 0)
    c = lax.broadcasted_iota(jnp.int32, (LANE, LANE), 1)
    tri = jnp.where(r <= c, 1.0, 0.0).astype(BF16)
    ones = jnp.ones((LANE, LANE), BF16)
    rr = lax.broadcasted_iota(jnp.int32, (rows, rows), 0)
    cc = lax.broadcasted_iota(jnp.int32, (rows, rows), 1)
    blk_strict = jnp.where((rr // nr == cc // nr) & (cc < rr), 1.0, 0.0).astype(BF16)

    eq01 = jnp.where(eq, 1.0, 0.0).reshape(rows, LANE)
    eq_rank, _ = _excl_cumsum(eq01, tri, ones, blk_strict)
    sel = gt | (eq & (eq_rank.reshape(N_EXPERTS, nr, LANE) < need))
    sel01 = jnp.where(sel, 1.0, 0.0).reshape(rows, LANE)
    pos, rowoff = _excl_cumsum(sel01, tri, ones, blk_strict)
    pos_ref[0] = jnp.where(sel01 > 0.0, pos, -1.0).astype(jnp.int32)
    gate_ref[0] = jnp.where(sel01 > 0.0, aff, 0.0)
    off_ref[0] = rowoff.astype(jnp.int32)


def _select(aff_t, batch, seq):
    nr = seq // LANE
    rows = N_EXPERTS * nr
    cap = CAP_FACTOR * seq // N_EXPERTS
    aff3 = aff_t.reshape(batch, rows, LANE)
    spec = pl.BlockSpec((1, rows, LANE), lambda b: (b, 0, 0))
    kern = functools.partial(_select_kernel, nr=nr, cap=cap)
    return pl.pallas_call(
        kern,
        out_shape=(jax.ShapeDtypeStruct((batch, rows, LANE), jnp.int32),
                   jax.ShapeDtypeStruct((batch, rows, LANE), F32),
                   jax.ShapeDtypeStruct((batch, rows, LANE), jnp.int32)),
        grid=(batch,), in_specs=[spec], out_specs=(spec, spec, spec),
        compiler_params=_params(("parallel",)), name="expert_select",
    )(aff3)


TOK_BLK = 2 * LANE
WIN = 64


def _tok_lanes(ref, b0, row):
    two = ref[b0, pl.ds(row, 2), :]
    return jnp.concatenate([two[0:1], two[1:2]], axis=1)


def _moe_up_kernel(boff_ref, h2_ref, pos_ref, gate_ref, wg_ref, wu_ref, hid_ref, xs, gr,
                   *, nblk, cap, ftile):
    b = pl.program_id(0)
    e = pl.program_id(1)
    base = (b * N_EXPERTS + e) * (nblk + 1)
    xs[...] = jnp.zeros_like(xs)
    gr[...] = jnp.zeros_like(gr)
    riota = lax.broadcasted_iota(jnp.int32, (WIN, TOK_BLK), 0)

    def block(j, carry):
        off = boff_ref[base + j]
        cnt = boff_ref[base + j + 1] - off
        r0 = (off // 8) * 8
        nwin = jnp.where(cnt > 0, (off + cnt - r0 + WIN - 1) // WIN, 0)
        prow = _tok_lanes(pos_ref, 0, 2 * j)
        grow = _tok_lanes(gate_ref, 0, 2 * j)
        toks = h2_ref[0, pl.ds(pl.multiple_of(j * TOK_BLK, TOK_BLK), TOK_BLK), :]

        def window(k, c2):
            rb = pl.multiple_of(r0 + k * WIN, 8)
            hit = (riota + rb) == prow
            onehot = jnp.where(hit, 1.0, 0.0).astype(BF16)
            xs[pl.ds(rb, WIN), :] += jnp.dot(onehot, toks, preferred_element_type=F32)
            g = jnp.sum(jnp.where(hit, grow, 0.0), axis=1, keepdims=True)
            gr[pl.ds(rb, WIN), :] += jnp.broadcast_to(g, (WIN, LANE))
            return c2

        lax.fori_loop(1, nwin, window, window(0, 0))
        return carry

    lax.fori_loop(0, nblk, block, 0)
    xsb = xs[0:cap, :].astype(BF16)
    gate = jnp.concatenate([gr[0:cap, :]] * (ftile // LANE), axis=1)
    for f in range(D_EXPERT // ftile):
        fs = slice(f * ftile, (f + 1) * ftile)
        g = jnp.dot(xsb, wg_ref[0, 0, :, fs].astype(BF16), preferred_element_type=F32)
        u = jnp.dot(xsb, wu_ref[0, 0, :, fs].astype(BF16), preferred_element_type=F32)
        hid_ref[0, 0, :, fs] = (g / (1.0 + jnp.exp(-g)) * u * gate).astype(BF16)


def _moe_up(boff, h2, pos, gate, wg, wu, layer, batch, seq):
    nr = seq // LANE
    nblk = seq // TOK_BLK
    cap = CAP_FACTOR * seq // N_EXPERTS
    kern = functools.partial(_moe_up_kernel, nblk=nblk, cap=cap, ftile=512)
    grid_spec = pltpu.PrefetchScalarGridSpec(
        num_scalar_prefetch=1, grid=(batch, N_EXPERTS),
        in_specs=[pl.BlockSpec((1, seq, D_MODEL), lambda b, e, s: (b, 0, 0),
                               pipeline_mode=pl.Buffered(1)),
                  pl.BlockSpec((1, nr, LANE), lambda b, e, s: (b, e, 0)),
                  pl.BlockSpec((1, nr, LANE), lambda b, e, s: (b, e, 0)),
                  pl.BlockSpec((1, 1, D_MODEL, D_EXPERT), lambda b, e, s: (layer, e, 0, 0)),
                  pl.BlockSpec((1, 1, D_MODEL, D_EXPERT), lambda b, e, s: (layer, e, 0, 0))],
        out_specs=pl.BlockSpec((1, 1, cap, D_EXPERT), lambda b, e, s: (b, e, 0, 0)),
        scratch_shapes=[pltpu.VMEM((cap + WIN, D_MODEL), F32),
                        pltpu.VMEM((cap + WIN, LANE), F32)])
    return pl.pallas_call(
        kern, out_shape=jax.ShapeDtypeStruct((batch, N_EXPERTS, cap, D_EXPERT), BF16),
        grid_spec=grid_spec,
        compiler_params=_params(("arbitrary", "arbitrary")), name="moe_gather_up",
    )(boff, h2, pos, gate, wg, wu)


GROUP = 4


def _moe_down_kernel(boff_ref, hid_ref, wd_ref, pos_ref, x1_ref, out_ref, ys,
                     *, nblk, nr, cap):
    b = pl.program_id(0)
    e = pl.program_id(2)

    @pl.when(e == 0)
    def _():
        ys[:, cap:, :] = jnp.zeros((N_EXPERTS, WIN, ys.shape[2]), ys.dtype)

    ys[e, 0:cap, :] = jnp.dot(hid_ref[0, 0], wd_ref[0, 0].astype(BF16), preferred_element_type=F32
                              ).astype(ys.dtype)

    @pl.when(e == N_EXPERTS - 1)
    def _():
        riota = lax.broadcasted_iota(jnp.int32, (WIN, TOK_BLK), 0)

        def block(j, carry):
            rows = pl.ds(pl.multiple_of(j * TOK_BLK, TOK_BLK), TOK_BLK)
            acc = x1_ref[0, rows, :]

            def group_pass(g0):
                starts, prows, npass = [], [], 0
                for e2 in range(g0, g0 + GROUP):
                    base = (b * N_EXPERTS + e2) * (nblk + 1)
                    off = boff_ref[base + j]
                    cnt = boff_ref[base + j + 1] - off
                    r0 = (off // 16) * 16
                    nwin = jnp.where(cnt > 0, (off + cnt - r0 + WIN - 1) // WIN, 0)
                    npass = jnp.maximum(npass, nwin)
                    starts.append(r0)
                    prows.append(_tok_lanes(pos_ref, 0, e2 * nr + 2 * j))

                def one_pass(k, a):
                    hots, vals = [], []
                    for i2, e2 in enumerate(range(g0, g0 + GROUP)):
                        rb = pl.multiple_of(jnp.minimum(starts[i2] + k * WIN, cap), 16)
                        hots.append(jnp.where((riota + rb) == prows[i2], 1.0, 0.0).astype(BF16))
                        vals.append(ys[e2, pl.ds(rb, WIN), :])
                    return a + lax.dot_general(
                        jnp.concatenate(hots, axis=0), jnp.concatenate(vals, axis=0),
                        (((0,), (0,)), ((), ())), preferred_element_type=F32)

                return one_pass, npass

            passes = [group_pass(g0) for g0 in range(0, N_EXPERTS, GROUP)]
            for one_pass, _ in passes:
                acc = one_pass(0, acc)
            for one_pass, npass in passes:
                acc = lax.fori_loop(1, npass, one_pass, acc)
            out_ref[0, rows, :] = acc
            return carry

        lax.fori_loop(0, nblk, block, 0)


def _moe_down(boff, hid, wd, pos, x1, layer, batch, seq, td):
    nr = seq // LANE
    nblk = seq // TOK_BLK
    cap = CAP_FACTOR * seq // N_EXPERTS
    kern = functools.partial(_moe_down_kernel, nblk=nblk, nr=nr, cap=cap)
    grid_spec = pltpu.PrefetchScalarGridSpec(
        num_scalar_prefetch=1, grid=(batch, D_MODEL // td, N_EXPERTS),
        in_specs=[pl.BlockSpec((1, 1, cap, D_EXPERT), lambda b, d, e, s: (b, e, 0, 0)),
                  pl.BlockSpec((1, 1, D_EXPERT, td), lambda b, d, e, s: (layer, e, 0, d)),
                  pl.BlockSpec((1, N_EXPERTS * nr, LANE), lambda b, d, e, s: (b, 0, 0)),
                  pl.BlockSpec((1, seq, td), lambda b, d, e, s: (b, 0, d))],
        out_specs=pl.BlockSpec((1, seq, td), lambda b, d, e, s: (b, 0, d)),
        scratch_shapes=[pltpu.VMEM((N_EXPERTS, cap + WIN, td), BF16)])
    return pl.pallas_call(
        kern, out_shape=jax.ShapeDtypeStruct((batch, seq, D_MODEL), F32),
        grid_spec=grid_spec,
        compiler_params=_params(("arbitrary", "arbitrary", "arbitrary")),
        name="moe_down_scatter",
    )(boff, hid, wd, pos, x1)


def _block_offsets(rowoff, batch, seq):
    nr = seq // LANE
    cap = CAP_FACTOR * seq // N_EXPERTS
    per_row = rowoff[:, :, 0].reshape(batch, N_EXPERTS, nr)
    step = TOK_BLK // LANE
    offs = jnp.concatenate([per_row[:, :, ::step],
                            jnp.full((batch, N_EXPERTS, 1), cap, jnp.int32)], axis=2)
    return offs.reshape(-1)


def _delta_params(conv_w, a_log, dt_bias):
    cw8 = jnp.concatenate([conv_w.astype(F32),
                           jnp.zeros((HALO - A_CONV, conv_w.shape[1]), F32)], axis=0)
    pad_l = jnp.zeros((N_INST,), F32)
    pad_r = jnp.zeros((LANE - 2 * N_INST,), F32)
    aneg = jnp.concatenate([pad_l, -jnp.exp(a_log.astype(F32)).reshape(-1), pad_r])[None, :]
    dtb = jnp.concatenate([pad_l, dt_bias.astype(F32).reshape(-1), pad_r])[None, :]
    return cw8, aneg, dtb


def _reorder_w_in(w):
    d = w.shape[0]
    gates = w[:, PA_W:PA_W + 4 * A_HEADS]
    rest = w[:, PA_W + 4 * A_HEADS:]
    pad = jnp.zeros((d, PROJ_W - OFF_G - 4 * A_HEADS), w.dtype)
    return jnp.concatenate([w[:, :PA_W], rest, gates, pad], axis=1)


def _qk_gains(gq, gk):
    scale = (HD ** -0.5) * math.log2(math.e)
    return jnp.concatenate([jnp.tile(gq.astype(F32) * scale, N_Q_HEADS),
                            jnp.tile(gk.astype(F32), N_KV_HEADS)])[None, :]


def _tiles(seq):
    tm = min(512, seq)
    return dict(tm=tm, tq=min(256, seq), tq_dense=min(256, seq), tk_dense=min(2048, seq),
                tk_win=min(256 + 2 * WINDOW, seq),
                tt=tm, cb=tm // A_CHUNK, td=256)


def _layer(x, p, l, tabs, batch, seq):
    ts = _tiles(seq)
    x2 = x.reshape(batch * seq, D_MODEL)
    w_re = _reorder_w_in(p["w_in"][l]).astype(BF16)
    gain_b = _qk_gains(p["b_qnorm_g"][l], p["b_knorm_g"][l])
    gain_c = _qk_gains(p["c_qnorm_g"][l], p["c_knorm_g"][l])
    pa, gb, qb, kb, vb, qc, kc, vc = _inproj(
        x2, p["norm1_g"][l].astype(F32)[None, :], w_re, gain_b, gain_c, tabs, seq, ts["tm"])

    cw8, aneg, dtb = _delta_params(p["conv_w"][l], p["a_log"][l], p["dt_bias"][l])
    u, w, qg, kd, a, dl = _delta_prep(pa, gb, cw8, aneg, dtb, seq, ts["tt"])
    o_f, o_b = _delta_scan(u, w, qg, kd, a, dl, batch, seq, ts["cb"])

    sinks = p["b_sink"][l].astype(F32) * math.log2(math.e)
    wnd = _attention(qb, kb, vb, sinks, batch, seq, ts["tq"], ts["tk_win"], True)
    dns = _attention(qc, kc, vc, sinks, batch, seq, ts["tq_dense"], ts["tk_dense"], False)

    x1, h2, aff_t = _outproj(
        o_f, o_b, pa, wnd, dns, p["a_norm_g"][l].astype(F32)[None, :],
        p["w_out"][l].astype(BF16), x2, p["norm2_g"][l].astype(F32)[None, :],
        p["w_router"][l].astype(F32).T, batch, seq, ts["tm"])

    pos, gate, rowoff = _select(aff_t, batch, seq)
    boff = _block_offsets(rowoff, batch, seq)
    hid = _moe_up(boff, h2.reshape(batch, seq, D_MODEL), pos, gate,
                  p["w_gate"], p["w_up"], l, batch, seq)
    return _moe_down(boff, hid, p["w_down"], pos,
                     x1.reshape(batch, seq, D_MODEL), l, batch, seq, ts["td"])


def kernel(x, norm1_g, w_in, conv_w, a_log, dt_bias, a_norm_g, b_qnorm_g, b_knorm_g, b_sink,
           c_qnorm_g, c_knorm_g, w_out, norm2_g, w_router, w_gate, w_up, w_down):
    batch, seq, _ = x.shape
    p = dict(norm1_g=norm1_g, w_in=w_in, conv_w=conv_w, a_log=a_log, dt_bias=dt_bias,
             a_norm_g=a_norm_g, b_qnorm_g=b_qnorm_g, b_knorm_g=b_knorm_g, b_sink=b_sink,
             c_qnorm_g=c_qnorm_g, c_knorm_g=c_knorm_g, w_out=w_out, norm2_g=norm2_g,
             w_router=w_router, w_gate=w_gate, w_up=w_up, w_down=w_down)
    tabs = _rope_tables(seq)
    for l in range(w_in.shape[0]):
        x = _layer(x, p, l, tabs, batch, seq)
    return x
```

```python
import functools
import math

import jax
import jax.numpy as jnp
from jax import lax
from jax.experimental import pallas as pl
from jax.experimental.pallas import tpu as pltpu

F32 = jnp.float32
BF16 = jnp.bfloat16

D_MODEL = 1024
GRID_W = 64
EPS = 1e-6
ROPE_THETA = 10000.0
A_HEADS = 4
A_DK = 128
A_DV = 128
A_CONV = 5
A_CHUNK = 64
HD = 64
N_Q_HEADS = 4
N_KV_HEADS = 2
WINDOW = 128
N_EXPERTS = 16
CAP_FACTOR = 2
D_EXPERT = 1024

QKV_A = 3 * A_HEADS * A_DK
PA_W = QKV_A + A_HEADS * A_DV
ATT_W = (N_Q_HEADS + 2 * N_KV_HEADS) * HD
QK_W = (N_Q_HEADS + N_KV_HEADS) * HD
OFF_B = PA_W
OFF_C = PA_W + ATT_W
OFF_G = PA_W + 2 * ATT_W
LANE = 128
PROJ_W = OFF_G + LANE

VMEM_LIMIT = 56 * 1024 * 1024


def _params(sem, vmem=None):
    return pltpu.CompilerParams(dimension_semantics=sem,
                                vmem_limit_bytes=vmem or VMEM_LIMIT)


def _group_sumsq(v, width):
    n = v.shape[-1]
    r = lax.broadcasted_iota(jnp.int32, (n, n), 0) // width
    c = lax.broadcasted_iota(jnp.int32, (n, n), 1) // width
    ones = jnp.where(r == c, 1.0, 0.0).astype(BF16)
    sq = v * v
    hi = sq.astype(BF16)
    lo = (sq - hi.astype(F32)).astype(BF16)
    return (jnp.dot(hi, ones, preferred_element_type=F32)
            + jnp.dot(lo, ones, preferred_element_type=F32))


def _dup_halves(v):
    lane = lax.broadcasted_iota(jnp.int32, v.shape, 1)
    sw = pltpu.roll(v, 64, 1)
    lo = lane < 64
    return jnp.concatenate([jnp.where(lo, v, sw), jnp.where(lo, sw, v)], axis=1)


def _with_ones(v):
    lane = lax.broadcasted_iota(jnp.int32, v.shape, 1)
    lo = lane < 64
    return jnp.concatenate([jnp.where(lo, v, 1.0),
                            jnp.where(lo, pltpu.roll(v, 64, 1), 1.0)], axis=1)


def _inproj_kernel(x_ref, g1_ref, w_ref, gain_b_ref, gain_c_ref,
                   cosb_ref, sinb_ref, cosc_ref, sinc_ref,
                   pa_ref, gb_ref, qb_ref, kb_ref, vb_ref, qc_ref, kc_ref, vc_ref):
    x = x_ref[...]
    ms = jnp.mean(x * x, axis=-1, keepdims=True)
    h = (x * lax.rsqrt(ms + EPS) * g1_ref[...]).astype(BF16)
    pa_ref[...] = jnp.dot(h, w_ref[:, 0:PA_W], preferred_element_type=F32)
    gb_ref[...] = jnp.dot(h, w_ref[:, OFF_G:PROJ_W], preferred_element_type=F32)

    def softmax_mixer(off, gain_ref, cos_ref, sin_ref, half, q_ref, k_ref, v_ref):
        y = jnp.dot(h, w_ref[:, off:off + ATT_W], preferred_element_type=F32)
        qk = y[:, :QK_W]
        ss = _group_sumsq(qk, HD)
        n = qk * lax.rsqrt(ss * (1.0 / HD) + EPS) * gain_ref[...]
        cos = jnp.concatenate([cos_ref[...]] * (QK_W // LANE), axis=1)
        sin = jnp.concatenate([sin_ref[...]] * (QK_W // LANE), axis=1)
        lane = lax.broadcasted_iota(jnp.int32, n.shape, 1)
        first = (lane % (2 * half)) < half
        rot = jnp.where(first, pltpu.roll(n, QK_W - half, 1), pltpu.roll(n, half, 1))
        r = n * cos + rot * sin
        q_ref[...] = r[:, :N_Q_HEADS * HD].astype(BF16)
        k_ref[...] = _dup_halves(r[:, N_Q_HEADS * HD:QK_W]).astype(BF16)
        v_ref[...] = _with_ones(y[:, QK_W:ATT_W]).astype(BF16)

    softmax_mixer(OFF_B, gain_b_ref, cosb_ref, sinb_ref, HD // 2, qb_ref, kb_ref, vb_ref)
    softmax_mixer(OFF_C, gain_c_ref, cosc_ref, sinc_ref, HD // 4, qc_ref, kc_ref, vc_ref)


def _inproj(x2, g1, w_re, gain_b, gain_c, tabs, seq, tm):
    t = x2.shape[0]
    nt = seq // tm
    row = lambda i: (i, 0)
    fixed = lambda i: (0, 0)
    tab = lambda i: (i % nt, 0)
    kv_w = 2 * N_KV_HEADS * HD
    out_shape = (
        jax.ShapeDtypeStruct((t, PA_W), F32),
        jax.ShapeDtypeStruct((t, LANE), F32),
        jax.ShapeDtypeStruct((t, N_Q_HEADS * HD), BF16),
        jax.ShapeDtypeStruct((t, kv_w), BF16),
        jax.ShapeDtypeStruct((t, kv_w), BF16),
        jax.ShapeDtypeStruct((t, N_Q_HEADS * HD), BF16),
        jax.ShapeDtypeStruct((t, kv_w), BF16),
        jax.ShapeDtypeStruct((t, kv_w), BF16),
    )
    in_specs = [
        pl.BlockSpec((tm, D_MODEL), row),
        pl.BlockSpec((1, D_MODEL), fixed),
        pl.BlockSpec((D_MODEL, PROJ_W), fixed),
        pl.BlockSpec((1, QK_W), fixed),
        pl.BlockSpec((1, QK_W), fixed),
    ] + [pl.BlockSpec((tm, LANE), tab)] * 4
    out_specs = tuple(pl.BlockSpec((tm, s.shape[1]), row) for s in out_shape)
    return pl.pallas_call(
        _inproj_kernel, out_shape=out_shape, grid=(t // tm,),
        in_specs=in_specs, out_specs=out_specs,
        compiler_params=_params(("parallel",)), name="inproj",
    )(x2, g1, w_re, gain_b, gain_c, *tabs)


def _rope_tables(seq):
    def tables(pos, dim):
        inv = ROPE_THETA ** (-jnp.arange(0, dim, 2, dtype=F32) / dim)
        ang = pos.astype(F32)[:, None] * inv[None, :]
        cos = jnp.concatenate([jnp.cos(ang), jnp.cos(ang)], axis=-1)
        sin = jnp.concatenate([-jnp.sin(ang), jnp.sin(ang)], axis=-1)
        return cos, sin
    pos = jnp.arange(seq)
    cos1, sin1 = tables(pos, HD)
    cosr, sinr = tables(pos // GRID_W, HD // 2)
    cosc, sinc = tables(pos % GRID_W, HD // 2)
    cos2 = jnp.concatenate([cosr, cosc], axis=-1)
    sin2 = jnp.concatenate([sinr, sinc], axis=-1)
    two = lambda a: jnp.concatenate([a, a], axis=-1)
    return two(cos1), two(sin1), two(cos2), two(sin2)


NEG_BIG = -1e30


def _attn_kernel(sink_ref, q_ref, k_ref, v_ref, o_ref, *, tq, tk, seq, window):
    qi = pl.program_id(1)
    q0 = qi * tq
    lane = lax.broadcasted_iota(jnp.int32, (tq, LANE), 1)
    row2 = lax.broadcasted_iota(jnp.int32, (2 * tq, 1), 0)
    if window:
        qpos = q0 + row2 % tq
    groups = range(N_KV_HEADS)
    cols = [slice(g * LANE, (g + 1) * LANE) for g in groups]
    q2s, init = [], []
    for g in groups:
        qg = q_ref[:, cols[g]]
        zero = jnp.zeros_like(qg)
        q2s.append(jnp.concatenate([jnp.where(lane < HD, qg, zero),
                                    jnp.where(lane >= HD, qg, zero)], axis=0))
        lane2 = lax.broadcasted_iota(jnp.int32, (2 * tq, LANE), 1)
        if window:
            m0 = jnp.where(row2 < tq, sink_ref[2 * g], sink_ref[2 * g + 1])
            a0 = jnp.where(lane2 >= HD, 1.0, 0.0)
        else:
            m0 = jnp.full((2 * tq, 1), NEG_BIG, F32)
            a0 = jnp.zeros((2 * tq, LANE), F32)
        init.append((m0, a0))

    def step(k0, carry):
        out = []
        for g in groups:
            m, acc = carry[g]
            kc = k_ref[pl.ds(k0, tk), cols[g]]
            vc = v_ref[pl.ds(k0, tk), cols[g]]
            s = lax.dot_general(q2s[g], kc, (((1,), (1,)), ((), ())),
                                preferred_element_type=F32)
            if window:
                kpos = k0 + lax.broadcasted_iota(jnp.int32, (1, tk), 1)
                s = jnp.where(jnp.abs(qpos - kpos) <= WINDOW, s, NEG_BIG)
            m_new = jnp.maximum(m, jnp.max(s, axis=1, keepdims=True))
            alpha = jnp.exp2(m - m_new)
            p = jnp.exp2((s - m_new).astype(BF16))
            acc = alpha * acc + jnp.dot(p, vc, preferred_element_type=F32)
            out.append((m_new, acc))
        return tuple(out)

    if window:
        res = step(pl.multiple_of(jnp.clip(q0 - WINDOW, 0, seq - tk), LANE), tuple(init))
    else:
        res = lax.fori_loop(0, seq // tk,
                            lambda c, carry: step(pl.multiple_of(c * tk, tk), carry),
                            tuple(init))
    for g in groups:
        acc = res[g][1]
        o = acc / pltpu.roll(acc, HD, 1)
        o_ref[:, cols[g]] = jnp.where(lane < HD, o[:tq],
                                      pltpu.roll(o[tq:], HD, 1)).astype(o_ref.dtype)


def _attention(q, kd, vd, sinks, batch, seq, tq, tk, window):
    t = q.shape[0]
    nq = seq // tq
    width = N_Q_HEADS * HD
    kern = functools.partial(_attn_kernel, tq=tq, tk=tk, seq=seq, window=window)
    return pl.pallas_call(
        kern, out_shape=jax.ShapeDtypeStruct((t, width), BF16), grid=(batch, nq),
        in_specs=[pl.BlockSpec(memory_space=pltpu.SMEM),
                  pl.BlockSpec((tq, width), lambda b, i: (b * nq + i, 0)),
                  pl.BlockSpec((seq, width), lambda b, i: (b, 0)),
                  pl.BlockSpec((seq, width), lambda b, i: (b, 0))],
        out_specs=pl.BlockSpec((tq, width), lambda b, i: (b * nq + i, 0)),
        compiler_params=_params(("parallel", "parallel")),
        name="window_attention" if window else "dense_attention",
    )(sinks, q, kd, vd)


N_INST = 2 * A_HEADS
HALO = 8
HI = lax.Precision.HIGHEST


def _dot_hi(a, b):
    return jnp.dot(a, b, preferred_element_type=F32, precision=HI)


def _dot_bf(a, b):
    return jnp.dot(a.astype(BF16), b.astype(BF16), preferred_element_type=F32)


def _block_diag(m, reps, blk_r, blk_c):
    tall = jnp.concatenate([m] * reps, axis=0)
    r = lax.broadcasted_iota(jnp.int32, tall.shape, 0) // blk_r
    c = lax.broadcasted_iota(jnp.int32, tall.shape, 1) // blk_c
    return jnp.where(r == c, tall, jnp.zeros_like(tall))


def _unit_tri_inverse_stacked(low, reps):
    n = low.shape[0]
    r = lax.broadcasted_iota(jnp.int32, low.shape, 0)
    c = lax.broadcasted_iota(jnp.int32, low.shape, 1) % n
    p = -low
    x = jnp.where(r == c, 1.0, 0.0) + p
    for _ in range(int(math.log2(n)) - 1):
        p = jnp.dot(p.astype(BF16), _block_diag(p.astype(BF16), reps, n, n),
                    preferred_element_type=F32)
        x = x + jnp.dot(x.astype(BF16), _block_diag(p.astype(BF16), reps, n, n),
                        preferred_element_type=F32)
    return x


def _lane_blocks(pieces, n, width):
    total = len(pieces) * width
    lane = lax.broadcasted_iota(jnp.int32, (n, total), 1) // width
    out = jnp.broadcast_to(pieces[-1], (n, total))
    for j in range(len(pieces) - 2, -1, -1):
        out = jnp.where(lane == j, jnp.broadcast_to(pieces[j], (n, total)), out)
    return out


def _delta_prep_kernel(prev_ref, cur_ref, next_ref, gb_ref, cw_ref, aneg_ref, dtb_ref,
                       u_ref, w_ref, qg_ref, kd_ref, a_ref, dl_ref,
                       qs, ks, vs, gs, bs, ext, *, tt, tiles_per_seq):
    i = pl.program_id(0) % tiles_per_seq
    ext[0:HALO, :] = jnp.where(i == 0, 0.0, prev_ref[...])
    ext[HALO:HALO + tt, :] = cur_ref[...]
    ext[HALO + tt:, :] = jnp.where(i == tiles_per_seq - 1, 0.0, next_ref[...])
    y = jnp.zeros((tt, QKV_A), F32)
    for j in range(A_CONV):
        s = HALO - A_CONV // 2 + j
        y = y + ext[s:s + tt, :] * cw_ref[j:j + 1, :]
    y = y / (1.0 + jnp.exp(-y))
    hw = A_HEADS * A_DK
    for h in range(A_HEADS):
        for src, dst, scale in ((0, qs, A_DK ** -0.5), (hw, ks, 1.0)):
            xh = y[:, src + h * A_DK: src + (h + 1) * A_DK]
            ss = jnp.sum(xh * xh, axis=1, keepdims=True)
            dst[:, h * A_DK:(h + 1) * A_DK] = xh * (lax.rsqrt(ss + EPS) * scale)
    vs[...] = y[:, 2 * hw:]
    gbv = gb_ref[...]
    bs[...] = 1.0 / (1.0 + jnp.exp(-gbv))
    xg = gbv + dtb_ref[...]
    softplus = jnp.maximum(xg, 0.0) + jnp.log1p(jnp.exp(-jnp.abs(xg)))
    gs[...] = aneg_ref[...] * softplus

    n = A_CHUNK
    nh = A_HEADS
    r = lax.broadcasted_iota(jnp.int32, (n, n), 0)
    c = lax.broadcasted_iota(jnp.int32, (n, n), 1)
    r4 = lax.broadcasted_iota(jnp.int32, (n, nh * n), 0)
    c4 = lax.broadcasted_iota(jnp.int32, (n, nh * n), 1) % n
    rt = lax.broadcasted_iota(jnp.int32, (nh * n, n), 0) % n
    ct = lax.broadcasted_iota(jnp.int32, (nh * n, n), 1)

    def chunk(ci, carry):
        r0 = pl.multiple_of(ci * n, n)
        rows = pl.ds(r0, n)
        g8 = gs[rows, :]
        b8 = bs[rows, :]
        q_all = qs[rows, :]
        k_all = ks[rows, :]
        v_all = vs[rows, :]
        gtot = jnp.sum(g8, axis=0, keepdims=True)
        dl_ref[pl.ds(ci, 1), :] = jnp.exp(gtot)
        k_bd = _block_diag(k_all.astype(BF16), nh, n, A_DK)
        for d in range(2):
            fwd = d == 0
            tri = jnp.where((c <= r) if fwd else (c >= r), 1.0, 0.0)
            tri_t4 = jnp.where((ct <= rt) if fwd else (ct >= rt), 1.0, 0.0)
            keep = (c4 <= r4) if fwd else (c4 >= r4)
            strict = (c4 < r4) if fwd else (c4 > r4)
            gc_all = _dot_hi(tri, g8)
            gc_row = lax.dot_general(g8, tri_t4, (((0,), (1,)), ((), ())),
                                     preferred_element_type=F32, precision=HI)
            lanes = [N_INST + d * nh + h for h in range(nh)]
            gcs = [gc_all[:, gl:gl + 1] for gl in lanes]
            egs = [jnp.exp(gc) for gc in gcs]
            betas = [b8[:, d * nh + h:d * nh + h + 1] for h in range(nh)]
            diff = (_lane_blocks(gcs, n, n)
                    - _lane_blocks([gc_row[gl:gl + 1, :] for gl in lanes], n, n))
            decay = jnp.where(keep, jnp.exp(jnp.where(keep, diff, 0.0)), 0.0)
            beta_w = _lane_blocks(betas, n, A_DK)
            eg_w = _lane_blocks(egs, n, A_DK)
            kb_all = k_all * beta_w
            both = lax.dot_general(jnp.concatenate([kb_all, q_all], axis=0).astype(BF16),
                                   k_bd, (((1,), (1,)), ((), ())),
                                   preferred_element_type=F32)
            tm = _unit_tri_inverse_stacked(jnp.where(strict, both[:n] * decay, 0.0), nh)
            vb_all = v_all * beta_w
            kbe_all = kb_all * eg_w
            rhs = jnp.concatenate(
                [jnp.concatenate([vb_all[:, h * A_DK:(h + 1) * A_DK],
                                  kbe_all[:, h * A_DK:(h + 1) * A_DK]], axis=1)
                 for h in range(nh)], axis=0)
            uw = jnp.dot(_block_diag(tm.astype(BF16), nh, n, n), rhs.astype(BF16),
                         preferred_element_type=F32)
            wide = slice(d * nh * A_DK, (d + 1) * nh * A_DK)
            for h in range(nh):
                o128 = slice((d * nh + h) * A_DK, (d * nh + h + 1) * A_DK)
                u_ref[rows, o128] = uw[h * n:(h + 1) * n, :A_DV]
                w_ref[rows, o128] = uw[h * n:(h + 1) * n, A_DV:]
            qg_ref[rows, wide] = q_all * eg_w
            kd_ref[rows, wide] = k_all * jnp.exp(
                _lane_blocks([gtot[:, gl:gl + 1] - gc for gl, gc in zip(lanes, gcs)], n, A_DK))
            a_ref[rows, d * nh * n:(d + 1) * nh * n] = both[n:] * decay
        return carry

    lax.fori_loop(0, tt // n, chunk, 0)


def _delta_prep(pa, gb, conv_w8, aneg, dtb, seq, tt):
    t = pa.shape[0]
    tps = seq // tt
    hb = tt // HALO
    last = t // HALO - 1
    wide = N_INST * A_DK
    kern = functools.partial(_delta_prep_kernel, tt=tt, tiles_per_seq=tps)
    big = jax.ShapeDtypeStruct((t, wide), F32)
    out_shape = (big, big, big, big,
                 jax.ShapeDtypeStruct((t, N_INST * A_CHUNK), F32),
                 jax.ShapeDtypeStruct((t // A_CHUNK, LANE), F32))
    row = lambda i: (i, 0)
    fixed = lambda i: (0, 0)
    return pl.pallas_call(
        kern, out_shape=out_shape, grid=(t // tt,),
        in_specs=[pl.BlockSpec((HALO, QKV_A), lambda i: (jnp.maximum(i * hb - 1, 0), 0)),
                  pl.BlockSpec((tt, QKV_A), row),
                  pl.BlockSpec((HALO, QKV_A), lambda i: (jnp.minimum((i + 1) * hb, last), 0)),
                  pl.BlockSpec((tt, LANE), row),
                  pl.BlockSpec((HALO, QKV_A), fixed),
                  pl.BlockSpec((1, LANE), fixed),
                  pl.BlockSpec((1, LANE), fixed)],
        out_specs=(pl.BlockSpec((tt, wide), row),) * 4
        + (pl.BlockSpec((tt, N_INST * A_CHUNK), row),
           pl.BlockSpec((tt // A_CHUNK, LANE), row)),
        scratch_shapes=[pltpu.VMEM((tt, A_HEADS * A_DK), F32)] * 3
        + [pltpu.VMEM((tt, LANE), F32)] * 2 + [pltpu.VMEM((tt + 2 * HALO, QKV_A), F32)],
        compiler_params=_params(("parallel",)), name="delta_prep",
    )(pa, pa, pa, gb, conv_w8, aneg, dtb)


def _delta_scan_kernel(uf, wf, qf, kf, af, dlf, ub, wb, qb, kb, ab, dlb,
                       of_ref, ob_ref, state, *, cb):
    @pl.when(pl.program_id(1) == 0)
    def _():
        state[...] = jnp.zeros_like(state)

    n = A_CHUNK
    dirs = ((uf, wf, qf, kf, af, dlf, of_ref), (ub, wb, qb, kb, ab, dlb, ob_ref))

    def chunk(ci, carry):
        for d, (u, w, q, k, a, dl, o) in enumerate(dirs):
            cc = ci if d == 0 else cb - 1 - ci
            rows = pl.ds(pl.multiple_of(cc * n, n), n)
            dlrow = dl[pl.ds(cc, 1), :]
            for h in range(A_HEADS):
                inst = d * A_HEADS + h
                hs = slice(h * A_DK, (h + 1) * A_DK)
                st = state[inst]
                stb = st.astype(BF16)
                v_new = u[rows, hs] - jnp.dot(w[rows, hs].astype(BF16), stb,
                                              preferred_element_type=F32)
                vb16 = v_new.astype(BF16)
                o[rows, hs] = (jnp.dot(q[rows, hs].astype(BF16), stb, preferred_element_type=F32)
                               + jnp.dot(a[rows, h * n:(h + 1) * n].astype(BF16), vb16,
                                         preferred_element_type=F32))
                upd = lax.dot_general(k[rows, hs].astype(BF16), vb16, (((0,), (0,)), ((), ())),
                                      preferred_element_type=F32)
                gl = N_INST + inst
                state[inst] = st * dlrow[:, gl:gl + 1] + upd
        return carry

    lax.fori_loop(0, cb, chunk, 0, unroll=True)


def _delta_scan(u, w, qg, kd, a, dl, batch, seq, cb):
    t = u.shape[0]
    tt = cb * A_CHUNK
    nb = seq // tt
    hw = A_HEADS * A_DV
    fwd = lambda b, i: (b * nb + i, 0)
    bwd = lambda b, i: (b * nb + nb - 1 - i, 1)
    bwd0 = lambda b, i: (b * nb + nb - 1 - i, 0)
    big_f = pl.BlockSpec((tt, hw), fwd)
    big_b = pl.BlockSpec((tt, hw), bwd)
    aw = A_HEADS * A_CHUNK
    kern = functools.partial(_delta_scan_kernel, cb=cb)
    out = jax.ShapeDtypeStruct((t, hw), F32)
    return pl.pallas_call(
        kern, out_shape=(out, out), grid=(batch, nb),
        in_specs=[big_f] * 4 + [pl.BlockSpec((tt, aw), fwd), pl.BlockSpec((cb, LANE), fwd)]
        + [big_b] * 4 + [pl.BlockSpec((tt, aw), bwd), pl.BlockSpec((cb, LANE), bwd0)],
        out_specs=(pl.BlockSpec((tt, hw), fwd), pl.BlockSpec((tt, hw), bwd0)),
        scratch_shapes=[pltpu.VMEM((N_INST, A_DK, A_DV), F32)],
        compiler_params=_params(("parallel", "arbitrary")), name="delta_scan",
    )(u, w, qg, kd, a, dl, u, w, qg, kd, a, dl)


def _outproj_kernel(of_ref, ob_ref, z_ref, wb_ref, dc_ref, ga_ref, wo_ref, x_ref, g2_ref, wr_ref,
                    x1_ref, h2_ref, aff_ref):
    o = of_ref[...] + ob_ref[...]
    z = z_ref[...]
    parts = []
    for h in range(A_HEADS):
        hs = slice(h * A_DV, (h + 1) * A_DV)
        oh = o[:, hs]
        ms = jnp.mean(oh * oh, axis=1, keepdims=True)
        zh = z[:, hs]
        parts.append((oh * lax.rsqrt(ms + EPS) * ga_ref[...] * (zh / (1.0 + jnp.exp(-zh)))
                      ).astype(BF16))
    mixed = jnp.concatenate(parts + [wb_ref[...], dc_ref[...]], axis=1)
    x1 = x_ref[...] + jnp.dot(mixed, wo_ref[...], preferred_element_type=F32)
    x1_ref[...] = x1
    ms = jnp.mean(x1 * x1, axis=1, keepdims=True)
    h2 = x1 * lax.rsqrt(ms + EPS) * g2_ref[...]
    h2_ref[...] = h2.astype(BF16)
    logits = lax.dot_general(wr_ref[...], h2, (((1,), (1,)), ((), ())),
                             preferred_element_type=F32, precision=HI)
    m = jnp.max(logits, axis=0, keepdims=True)
    p = jnp.exp(logits - m)
    aff_ref[0] = p / jnp.sum(p, axis=0, keepdims=True)


def _outproj(o_f, o_b, pa, wnd, dns, ga, w_out, x2, g2, wr_t, batch, seq, tm):
    t = x2.shape[0]
    nt = seq // tm
    row = lambda i: (i, 0)
    fixed = lambda i: (0, 0)
    hw = A_HEADS * A_DV
    aw = N_Q_HEADS * HD
    return pl.pallas_call(
        _outproj_kernel,
        out_shape=(jax.ShapeDtypeStruct((t, D_MODEL), F32),
                   jax.ShapeDtypeStruct((t, D_MODEL), BF16),
                   jax.ShapeDtypeStruct((batch, N_EXPERTS, seq), F32)),
        grid=(t // tm,),
        in_specs=[pl.BlockSpec((tm, hw), row), pl.BlockSpec((tm, hw), row),
                  pl.BlockSpec((tm, hw), lambda i: (i, QKV_A // hw)),
                  pl.BlockSpec((tm, aw), row), pl.BlockSpec((tm, aw), row),
                  pl.BlockSpec((1, A_DV), fixed),
                  pl.BlockSpec((D_MODEL, D_MODEL), fixed),
                  pl.BlockSpec((tm, D_MODEL), row),
                  pl.BlockSpec((1, D_MODEL), fixed),
                  pl.BlockSpec((N_EXPERTS, D_MODEL), fixed)],
        out_specs=(pl.BlockSpec((tm, D_MODEL), row), pl.BlockSpec((tm, D_MODEL), row),
                   pl.BlockSpec((1, N_EXPERTS, tm), lambda i: (i // nt, 0, i % nt))),
        compiler_params=_params(("parallel",)), name="outproj_router",
    )(o_f, o_b, pa, wnd, dns, ga, w_out, x2, g2, wr_t)


def _excl_cumsum(mask01, tri, ones, blk_strict):
    mb = mask01.astype(BF16)
    incl = jnp.dot(mb, tri, preferred_element_type=F32)
    tot = jnp.dot(mb, ones, preferred_element_type=F32)
    rowoff = jnp.dot(blk_strict, tot.astype(BF16), preferred_element_type=F32)
    return rowoff + incl - mask01, rowoff


def _select_kernel(aff_ref, pos_ref, gate_ref, off_ref, *, nr, cap):
    aff = aff_ref[0]
    rows = aff.shape[0]
    aff3 = aff.reshape(N_EXPERTS, nr, LANE)

    def count(mask):
        c = jnp.sum(jnp.where(mask, 1.0, 0.0), axis=2, keepdims=True)
        return jnp.sum(c, axis=1, keepdims=True)

    def as_float(word):
        return lax.bitcast_convert_type(word, F32)

    def search(i, thr):
        cand = thr | lax.shift_left(jnp.int32(1), 30 - i)
        return jnp.where(count(aff3 >= as_float(cand)) >= cap, cand, thr)

    thr = lax.fori_loop(0, 31, search, jnp.zeros((N_EXPERTS, 1, 1), jnp.int32))
    gt = aff3 >= as_float(thr + 1)
    eq = (aff3 >= as_float(thr)) & jnp.logical_not(gt)
    need = cap - count(gt)

    r = lax.broadcasted_iota(jnp.int32, (LANE, LANE), 0)
    c = lax.broadcasted_iota(jnp.int32, (LANE, LANE), 1)
    tri = jnp.where(r <= c, 1.0, 0.0).astype(BF16)
    ones = jnp.ones((LANE, LANE), BF16)
    rr = lax.broadcasted_iota(jnp.int32, (rows, rows), 0)
    cc = lax.broadcasted_iota(jnp.int32, (rows, rows), 1)
    blk_strict = jnp.where((rr // nr == cc // nr) & (cc < rr), 1.0, 0.0).astype(BF16)

    eq01 = jnp.where(eq, 1.0, 0.0).reshape(rows, LANE)
    eq_rank, _ = _excl_cumsum(eq01, tri, ones, blk_strict)
    sel = gt | (eq & (eq_rank.reshape(N_EXPERTS, nr, LANE) < need))
    sel01 = jnp.where(sel, 1.0, 0.0).reshape(rows, LANE)
    pos, rowoff = _excl_cumsum(sel01, tri, ones, blk_strict)
    pos_ref[0] = jnp.where(sel01 > 0.0, pos, -1.0).astype(jnp.int32)
    gate_ref[0] = jnp.where(sel01 > 0.0, aff, 0.0)
    off_ref[0] = rowoff.astype(jnp.int32)


def _select(aff_t, batch, seq):
    nr = seq // LANE
    rows = N_EXPERTS * nr
    cap = CAP_FACTOR * seq // N_EXPERTS
    aff3 = aff_t.reshape(batch, rows, LANE)
    spec = pl.BlockSpec((1, rows, LANE), lambda b: (b, 0, 0))
    kern = functools.partial(_select_kernel, nr=nr, cap=cap)
    return pl.pallas_call(
        kern,
        out_shape=(jax.ShapeDtypeStruct((batch, rows, LANE), jnp.int32),
                   jax.ShapeDtypeStruct((batch, rows, LANE), F32),
                   jax.ShapeDtypeStruct((batch, rows, LANE), jnp.int32)),
        grid=(batch,), in_specs=[spec], out_specs=(spec, spec, spec),
        compiler_params=_params(("parallel",)), name="expert_select",
    )(aff3)


TOK_BLK = 2 * LANE
WIN = 64


def _tok_lanes(ref, b0, row):
    two = ref[b0, pl.ds(row, 2), :]
    return jnp.concatenate([two[0:1], two[1:2]], axis=1)


def _moe_up_kernel(boff_ref, h2_ref, pos_ref, gate_ref, wg_ref, wu_ref, hid_ref, xs, gr,
                   *, nblk, cap, ftile):
    b = pl.program_id(0)
    e = pl.program_id(1)
    base = (b * N_EXPERTS + e) * (nblk + 1)
    xs[...] = jnp.zeros_like(xs)
    gr[...] = jnp.zeros_like(gr)
    riota = lax.broadcasted_iota(jnp.int32, (WIN, TOK_BLK), 0)

    def block(j, carry):
        off = boff_ref[base + j]
        cnt = boff_ref[base + j + 1] - off
        r0 = (off // 8) * 8
        nwin = jnp.where(cnt > 0, (off + cnt - r0 + WIN - 1) // WIN, 0)
        prow = _tok_lanes(pos_ref, 0, 2 * j)
        grow = _tok_lanes(gate_ref, 0, 2 * j)
        toks = h2_ref[0, pl.ds(pl.multiple_of(j * TOK_BLK, TOK_BLK), TOK_BLK), :]

        def window(k, c2):
            rb = pl.multiple_of(r0 + k * WIN, 8)
            hit = (riota + rb) == prow
            onehot = jnp.where(hit, 1.0, 0.0).astype(BF16)
            xs[pl.ds(rb, WIN), :] += jnp.dot(onehot, toks, preferred_element_type=F32)
            g = jnp.sum(jnp.where(hit, grow, 0.0), axis=1, keepdims=True)
            gr[pl.ds(rb, WIN), :] += jnp.broadcast_to(g, (WIN, LANE))
            return c2

        lax.fori_loop(1, nwin, window, window(0, 0))
        return carry

    lax.fori_loop(0, nblk, block, 0)
    xsb = xs[0:cap, :].astype(BF16)
    gate = jnp.concatenate([gr[0:cap, :]] * (ftile // LANE), axis=1)
    for f in range(D_EXPERT // ftile):
        fs = slice(f * ftile, (f + 1) * ftile)
        g = jnp.dot(xsb, wg_ref[0, 0, :, fs].astype(BF16), preferred_element_type=F32)
        u = jnp.dot(xsb, wu_ref[0, 0, :, fs].astype(BF16), preferred_element_type=F32)
        hid_ref[0, 0, :, fs] = (g / (1.0 + jnp.exp(-g)) * u * gate).astype(BF16)


def _moe_up(boff, h2, pos, gate, wg, wu, layer, batch, seq):
    nr = seq // LANE
    nblk = seq // TOK_BLK
    cap = CAP_FACTOR * seq // N_EXPERTS
    kern = functools.partial(_moe_up_kernel, nblk=nblk, cap=cap, ftile=512)
    grid_spec = pltpu.PrefetchScalarGridSpec(
        num_scalar_prefetch=1, grid=(batch, N_EXPERTS),
        in_specs=[pl.BlockSpec((1, seq, D_MODEL), lambda b, e, s: (b, 0, 0),
                               pipeline_mode=pl.Buffered(1)),
                  pl.BlockSpec((1, nr, LANE), lambda b, e, s: (b, e, 0)),
                  pl.BlockSpec((1, nr, LANE), lambda b, e, s: (b, e, 0)),
                  pl.BlockSpec((1, 1, D_MODEL, D_EXPERT), lambda b, e, s: (layer, e, 0, 0)),
                  pl.BlockSpec((1, 1, D_MODEL, D_EXPERT), lambda b, e, s: (layer, e, 0, 0))],
        out_specs=pl.BlockSpec((1, 1, cap, D_EXPERT), lambda b, e, s: (b, e, 0, 0)),
        scratch_shapes=[pltpu.VMEM((cap + WIN, D_MODEL), F32),
                        pltpu.VMEM((cap + WIN, LANE), F32)])
    return pl.pallas_call(
        kern, out_shape=jax.ShapeDtypeStruct((batch, N_EXPERTS, cap, D_EXPERT), BF16),
        grid_spec=grid_spec,
        compiler_params=_params(("arbitrary", "arbitrary")), name="moe_gather_up",
    )(boff, h2, pos, gate, wg, wu)


GROUP = 4


def _moe_down_kernel(boff_ref, hid_ref, wd_ref, pos_ref, x1_ref, out_ref, ys,
                     *, nblk, nr, cap):
    b = pl.program_id(0)
    e = pl.program_id(2)

    @pl.when(e == 0)
    def _():
        ys[:, cap:, :] = jnp.zeros((N_EXPERTS, WIN, ys.shape[2]), ys.dtype)

    ys[e, 0:cap, :] = jnp.dot(hid_ref[0, 0], wd_ref[0, 0].astype(BF16), preferred_element_type=F32
                              ).astype(ys.dtype)

    @pl.when(e == N_EXPERTS - 1)
    def _():
        riota = lax.broadcasted_iota(jnp.int32, (WIN, TOK_BLK), 0)

        def block(j, carry):
            rows = pl.ds(pl.multiple_of(j * TOK_BLK, TOK_BLK), TOK_BLK)
            acc = x1_ref[0, rows, :]

            def group_pass(g0):
                starts, prows, npass = [], [], 0
                for e2 in range(g0, g0 + GROUP):
                    base = (b * N_EXPERTS + e2) * (nblk + 1)
                    off = boff_ref[base + j]
                    cnt = boff_ref[base + j + 1] - off
                    r0 = (off // 16) * 16
                    nwin = jnp.where(cnt > 0, (off + cnt - r0 + WIN - 1) // WIN, 0)
                    npass = jnp.maximum(npass, nwin)
                    starts.append(r0)
                    prows.append(_tok_lanes(pos_ref, 0, e2 * nr + 2 * j))

                def one_pass(k, a):
                    hots, vals = [], []
                    for i2, e2 in enumerate(range(g0, g0 + GROUP)):
                        rb = pl.multiple_of(jnp.minimum(starts[i2] + k * WIN, cap), 16)
                        hots.append(jnp.where((riota + rb) == prows[i2], 1.0, 0.0).astype(BF16))
                        vals.append(ys[e2, pl.ds(rb, WIN), :])
                    return a + lax.dot_general(
                        jnp.concatenate(hots, axis=0), jnp.concatenate(vals, axis=0),
                        (((0,), (0,)), ((), ())), preferred_element_type=F32)

                return one_pass, npass

            passes = [group_pass(g0) for g0 in range(0, N_EXPERTS, GROUP)]
            for one_pass, _ in passes:
                acc = one_pass(0, acc)
            for one_pass, npass in passes:
                acc = lax.fori_loop(1, npass, one_pass, acc)
            out_ref[0, rows, :] = acc
            return carry

        lax.fori_loop(0, nblk, block, 0)


def _moe_down(boff, hid, wd, pos, x1, layer, batch, seq, td):
    nr = seq // LANE
    nblk = seq // TOK_BLK
    cap = CAP_FACTOR * seq // N_EXPERTS
    kern = functools.partial(_moe_down_kernel, nblk=nblk, nr=nr, cap=cap)
    grid_spec = pltpu.PrefetchScalarGridSpec(
        num_scalar_prefetch=1, grid=(batch, D_MODEL // td, N_EXPERTS),
        in_specs=[pl.BlockSpec((1, 1, cap, D_EXPERT), lambda b, d, e, s: (b, e, 0, 0)),
                  pl.BlockSpec((1, 1, D_EXPERT, td), lambda b, d, e, s: (layer, e, 0, d)),
                  pl.BlockSpec((1, N_EXPERTS * nr, LANE), lambda b, d, e, s: (b, 0, 0)),
                  pl.BlockSpec((1, seq, td), lambda b, d, e, s: (b, 0, d))],
        out_specs=pl.BlockSpec((1, seq, td), lambda b, d, e, s: (b, 0, d)),
        scratch_shapes=[pltpu.VMEM((N_EXPERTS, cap + WIN, td), BF16)])
    return pl.pallas_call(
        kern, out_shape=jax.ShapeDtypeStruct((batch, seq, D_MODEL), F32),
        grid_spec=grid_spec,
        compiler_params=_params(("arbitrary", "arbitrary", "arbitrary")),
        name="moe_down_scatter",
    )(boff, hid, wd, pos, x1)


def _block_offsets(rowoff, batch, seq):
    nr = seq // LANE
    cap = CAP_FACTOR * seq // N_EXPERTS
    per_row = rowoff[:, :, 0].reshape(batch, N_EXPERTS, nr)
    step = TOK_BLK // LANE
    offs = jnp.concatenate([per_row[:, :, ::step],
                            jnp.full((batch, N_EXPERTS, 1), cap, jnp.int32)], axis=2)
    return offs.reshape(-1)


def _delta_params(conv_w, a_log, dt_bias):
    cw8 = jnp.concatenate([conv_w.astype(F32),
                           jnp.zeros((HALO - A_CONV, conv_w.shape[1]), F32)], axis=0)
    pad_l = jnp.zeros((N_INST,), F32)
    pad_r = jnp.zeros((LANE - 2 * N_INST,), F32)
    aneg = jnp.concatenate([pad_l, -jnp.exp(a_log.astype(F32)).reshape(-1), pad_r])[None, :]
    dtb = jnp.concatenate([pad_l, dt_bias.astype(F32).reshape(-1), pad_r])[None, :]
    return cw8, aneg, dtb


def _reorder_w_in(w):
    d = w.shape[0]
    gates = w[:, PA_W:PA_W + 4 * A_HEADS]
    rest = w[:, PA_W + 4 * A_HEADS:]
    pad = jnp.zeros((d, PROJ_W - OFF_G - 4 * A_HEADS), w.dtype)
    return jnp.concatenate([w[:, :PA_W], rest, gates, pad], axis=1)


def _qk_gains(gq, gk):
    scale = (HD ** -0.5) * math.log2(math.e)
    return jnp.concatenate([jnp.tile(gq.astype(F32) * scale, N_Q_HEADS),
                            jnp.tile(gk.astype(F32), N_KV_HEADS)])[None, :]


def _tiles(seq):
    tm = min(512, seq)
    return dict(tm=tm, tq=min(256, seq), tq_dense=min(256, seq), tk_dense=min(2048, seq),
                tk_win=min(256 + 2 * WINDOW, seq),
                tt=tm, cb=tm // A_CHUNK, td=256)


def _layer(x, p, l, tabs, batch, seq):
    ts = _tiles(seq)
    x2 = x.reshape(batch * seq, D_MODEL)
    w_re = _reorder_w_in(p["w_in"][l]).astype(BF16)
    gain_b = _qk_gains(p["b_qnorm_g"][l], p["b_knorm_g"][l])
    gain_c = _qk_gains(p["c_qnorm_g"][l], p["c_knorm_g"][l])
    pa, gb, qb, kb, vb, qc, kc, vc = _inproj(
        x2, p["norm1_g"][l].astype(F32)[None, :], w_re, gain_b, gain_c, tabs, seq, ts["tm"])

    cw8, aneg, dtb = _delta_params(p["conv_w"][l], p["a_log"][l], p["dt_bias"][l])
    u, w, qg, kd, a, dl = _delta_prep(pa, gb, cw8, aneg, dtb, seq, ts["tt"])
    o_f, o_b = _delta_scan(u, w, qg, kd, a, dl, batch, seq, ts["cb"])

    sinks = p["b_sink"][l].astype(F32) * math.log2(math.e)
    wnd = _attention(qb, kb, vb, sinks, batch, seq, ts["tq"], ts["tk_win"], True)
    dns = _attention(qc, kc, vc, sinks, batch, seq, ts["tq_dense"], ts["tk_dense"], False)

    x1, h2, aff_t = _outproj(
        o_f, o_b, pa, wnd, dns, p["a_norm_g"][l].astype(F32)[None, :],
        p["w_out"][l].astype(BF16), x2, p["norm2_g"][l].astype(F32)[None, :],
        p["w_router"][l].astype(F32).T, batch, seq, ts["tm"])

    pos, gate, rowoff = _select(aff_t, batch, seq)
    boff = _block_offsets(rowoff, batch, seq)
    hid = _moe_up(boff, h2.reshape(batch, seq, D_MODEL), pos, gate,
                  p["w_gate"], p["w_up"], l, batch, seq)
    return _moe_down(boff, hid, p["w_down"], pos,
                     x1.reshape(batch, seq, D_MODEL), l, batch, seq, ts["td"])


def kernel(x, norm1_g, w_in, conv_w, a_log, dt_bias, a_norm_g, b_qnorm_g, b_knorm_g, b_sink,
           c_qnorm_g, c_knorm_g, w_out, norm2_g, w_router, w_gate, w_up, w_down):
    batch, seq, _ = x.shape
    p = dict(norm1_g=norm1_g, w_in=w_in, conv_w=conv_w, a_log=a_log, dt_bias=dt_bias,
             a_norm_g=a_norm_g, b_qnorm_g=b_qnorm_g, b_knorm_g=b_knorm_g, b_sink=b_sink,
             c_qnorm_g=c_qnorm_g, c_knorm_g=c_knorm_g, w_out=w_out, norm2_g=norm2_g,
             w_router=w_router, w_gate=w_gate, w_up=w_up, w_down=w_down)
    tabs = _rope_tables(seq)
    for l in range(w_in.shape[0]):
        x = _layer(x, p, l, tabs, batch, seq)
    return x
```
